```python
import jax, jax.numpy as jnp
from jax import lax
import numpy as np

D_MODEL = 1024
BATCH = 8
SEQ = 2048
DEPTH = 4
DEC_BATCH = 128
DEC_SEQ = 1
PAST_LEN = 16384
PAGE_SIZE = 128

CHUNK = 128
A_GROUPS = 4
D_A = D_MODEL // 2
A_GROUP_DIM = D_A // A_GROUPS
D_B = D_MODEL // 2
CONV_B_WIDTH = 3
D_C = D_MODEL // 2
CONV_C_WIDTH = 31
N_BRANCHES = 3
D_BR = D_MODEL // 2
D_IN = 2 * D_A + 3 * D_B + 2 * D_C + N_BRANCHES * D_MODEL
N_KEYS = 128
N_EXPERTS = N_KEYS * N_KEYS
PEER_HEADS = 8
PEER_TOPK = 16
D_KEY = 256
D_KEY_HALF = D_KEY // 2
PEER_BLOCK = 128
EPS = 1e-6

kernel_name = "gated_branch_chunkmlp_shortconv_conformer_peer_decode"


def rmsnorm(x, g):
    xf = x.astype(jnp.float32)
    y = xf * lax.rsqrt(jnp.mean(xf * xf, axis=-1, keepdims=True) + EPS)
    return (y * g.astype(jnp.float32)).astype(x.dtype)


def layernorm(x, g, b):
    xf = x.astype(jnp.float32)
    mu = jnp.mean(xf, axis=-1, keepdims=True)
    var = jnp.mean(jnp.square(xf - mu), axis=-1, keepdims=True)
    y = (xf - mu) * lax.rsqrt(var + EPS)
    return (y * g.astype(jnp.float32) + b.astype(jnp.float32)).astype(x.dtype)


def causal_dwconv(hist, z, w):
    width, ch = w.shape
    zz = jnp.concatenate([hist.astype(z.dtype), z], axis=1)
    y = lax.conv_general_dilated(zz, w[:, None, :].astype(z.dtype), window_strides=(1,), padding='VALID',
                                 dimension_numbers=('NWC', 'WIO', 'NWC'), feature_group_count=ch)
    return y, zz[:, -(width - 1):]


def chunk_spatial_gate(u, v, w_s, b_s):
    bsz, t, _ = v.shape
    pad = (-t) % CHUNK
    vp = jnp.pad(v, ((0, 0), (0, pad), (0, 0)))
    nc = vp.shape[1] // CHUNK
    vc = vp.reshape(bsz, nc, CHUNK, A_GROUPS, A_GROUP_DIM)
    mask = jnp.tril(jnp.ones((CHUNK, CHUNK), dtype=bool))
    w = jnp.where(mask[None], w_s, jnp.zeros_like(w_s)).astype(v.dtype)
    mixed = jnp.einsum('gts,bcsgd->bctgd', w, vc) + jnp.transpose(b_s).astype(v.dtype)[None, None, :, :, None]
    mixed = mixed.reshape(bsz, nc * CHUNK, D_A)[:, :t]
    return u * mixed


def peer(h, w_query, sub_keys, expert_u, expert_v):
    bsz, t, d = h.shape
    xf = h.reshape(bsz * t, d)
    n = xf.shape[0]
    pad = (-n) % PEER_BLOCK
    blocks = jnp.pad(xf, ((0, pad), (0, 0))).reshape(-1, PEER_BLOCK, d)

    def one_block(xb):
        q = (xb @ w_query).reshape(PEER_BLOCK, PEER_HEADS, 2, D_KEY_HALF)
        s = jnp.einsum('thpk,hpnk->thpn', q, sub_keys).astype(jnp.float32)
        s_top, i_top = lax.top_k(s, PEER_TOPK)
        cand = s_top[:, :, 0, :, None] + s_top[:, :, 1, None, :]
        cand_idx = i_top[:, :, 0, :, None] * N_KEYS + i_top[:, :, 1, None, :]
        cand = cand.reshape(PEER_BLOCK, PEER_HEADS, PEER_TOPK * PEER_TOPK)
        cand_idx = cand_idx.reshape(PEER_BLOCK, PEER_HEADS, PEER_TOPK * PEER_TOPK)
        best, pos = lax.top_k(cand, PEER_TOPK)
        idx = jnp.take_along_axis(cand_idx, pos, axis=-1)
        g = jax.nn.softmax(best, axis=-1)
        u = expert_u[idx]
        v = expert_v[idx]
        a = jax.nn.gelu(jnp.einsum('td,thkd->thk', xb, u))
        return jnp.einsum('thk,thkd->td', (g * a.astype(jnp.float32)).astype(v.dtype), v)

    out = lax.map(one_block, blocks).reshape(-1, d)[:n]
    return out.reshape(bsz, t, d).astype(h.dtype)


def trunk_layer(x, c, hist_b, hist_c, norm_g, ada_w, ada_b, w_in, ln_a_g, ln_a_b, w_spatial, b_spatial,
                conv_b_w, conv_c_w, conv_c_b, ln_c_g, ln_c_b, w_branch, w_out,
                peer_w_query, peer_sub_keys, peer_u, peer_v):
    bsz, t, d = x.shape
    mod = (jax.nn.silu(c) @ ada_w + ada_b)[:, None, :]
    shift1, scale1, gate1, shift2, scale2, gate2 = jnp.split(mod, 6, axis=-1)

    h = rmsnorm(x, norm_g[0]) * (1.0 + scale1) + shift1
    z = h @ w_in
    sizes = [D_A, D_A, D_B, D_B, D_B, D_C, D_C]
    cuts = [int(s) for s in np.cumsum(sizes)]
    ua, va, hb, bb, cb, ac, gc, zg = jnp.split(z, cuts, axis=-1)
    ua = jax.nn.gelu(ua)
    va = layernorm(jax.nn.gelu(va), ln_a_g, ln_a_b)
    ya = chunk_spatial_gate(ua, va, w_spatial, b_spatial)
    conv_in_b = cb * hb
    conv_out_b, new_hist_b = causal_dwconv(hist_b, conv_in_b, conv_b_w)
    yb = bb * conv_out_b
    glu = ac * jax.nn.sigmoid(gc)
    conv_out_c, new_hist_c = causal_dwconv(hist_c, glu, conv_c_w)
    yc = jax.nn.silu(layernorm(conv_out_c + conv_c_b, ln_c_g, ln_c_b))
    gates = jax.nn.sigmoid(zg).reshape(bsz, t, N_BRANCHES, d)
    merged = (gates[:, :, 0] * (ya @ w_branch[0]) + gates[:, :, 1] * (yb @ w_branch[1])
              + gates[:, :, 2] * (yc @ w_branch[2]))
    x = x + gate1 * (merged @ w_out)

    h2 = rmsnorm(x, norm_g[1]) * (1.0 + scale2) + shift2
    x = x + gate2 * peer(h2, peer_w_query, peer_sub_keys, peer_u, peer_v)
    return x, va, new_hist_b, new_hist_c


def setup_inputs(seed: int = 0) -> dict:
    key = jax.random.key(seed)
    ks = jax.random.split(key, 30)
    f32 = jnp.float32
    nrm = lambda k, shape, s: jax.random.normal(k, shape, f32) * s
    return {
        "x_prompt": nrm(ks[0], (BATCH, SEQ, D_MODEL), 1.0),
        "x_sample": nrm(ks[1], (DEC_BATCH, DEC_SEQ, D_MODEL), 1.0),
        "c_prompt": nrm(ks[2], (BATCH, D_MODEL), 1.0),
        "c_sample": nrm(ks[3], (DEC_BATCH, D_MODEL), 1.0),
        "state_conv_b": nrm(ks[4], (DEPTH, DEC_BATCH, CONV_B_WIDTH - 1, D_B), 0.5),
        "state_conv_c": nrm(ks[5], (DEPTH, DEC_BATCH, CONV_C_WIDTH - 1, D_C), 0.5),
        "norm_g": 1.0 + nrm(ks[6], (DEPTH, 2, D_MODEL), 0.1),
        "final_g": 1.0 + nrm(ks[7], (D_MODEL,), 0.1),
        "ada_w": nrm(ks[8], (DEPTH, D_MODEL, 6 * D_MODEL), 0.5 * D_MODEL ** -0.5),
        "ada_b": nrm(ks[9], (DEPTH, 6 * D_MODEL), 0.05),
        "w_in": nrm(ks[10], (DEPTH, D_MODEL, D_IN), D_MODEL ** -0.5),
        "ln_a_g": 1.0 + nrm(ks[11], (DEPTH, D_A), 0.1),
        "ln_a_b": nrm(ks[12], (DEPTH, D_A), 0.1),
        "w_spatial": nrm(ks[13], (DEPTH, A_GROUPS, CHUNK, CHUNK), CHUNK ** -0.5),
        "b_spatial": 1.0 + nrm(ks[14], (DEPTH, A_GROUPS, CHUNK), 0.1),
        "conv_b_w": nrm(ks[15], (DEPTH, CONV_B_WIDTH, D_B), CONV_B_WIDTH ** -0.5),
        "conv_c_w": nrm(ks[16], (DEPTH, CONV_C_WIDTH, D_C), CONV_C_WIDTH ** -0.5),
        "conv_c_b": nrm(ks[17], (DEPTH, D_C), 0.1),
        "ln_c_g": 1.0 + nrm(ks[18], (DEPTH, D_C), 0.1),
        "ln_c_b": nrm(ks[19], (DEPTH, D_C), 0.1),
        "w_branch": nrm(ks[20], (DEPTH, N_BRANCHES, D_BR, D_MODEL), D_BR ** -0.5),
        "w_out": nrm(ks[21], (DEPTH, D_MODEL, D_MODEL), D_MODEL ** -0.5),
        "peer_w_query": nrm(ks[22], (DEPTH, D_MODEL, PEER_HEADS * D_KEY), D_MODEL ** -0.5),
        "peer_sub_keys": nrm(ks[23], (DEPTH, PEER_HEADS, 2, N_KEYS, D_KEY_HALF), D_KEY_HALF ** -0.5),
        "peer_u": nrm(ks[24], (DEPTH, N_EXPERTS, D_MODEL), D_MODEL ** -0.5),
        "peer_v": nrm(ks[25], (DEPTH, N_EXPERTS, D_MODEL), PEER_HEADS ** -0.5),
    }


def reference(x_prompt, x_sample, c_prompt, c_sample, state_conv_b, state_conv_c,
              norm_g, final_g, ada_w, ada_b, w_in, ln_a_g, ln_a_b, w_spatial, b_spatial,
              conv_b_w, conv_c_w, conv_c_b, ln_c_g, ln_c_b, w_branch, w_out,
              peer_w_query, peer_sub_keys, peer_u, peer_v):
    bp = x_prompt.shape[0]
    xp, xs = x_prompt, x_sample
    zero_b = jnp.zeros((bp, CONV_B_WIDTH - 1, D_B), x_prompt.dtype)
    zero_c = jnp.zeros((bp, CONV_C_WIDTH - 1, D_C), x_prompt.dtype)
    v_s_list, hb_p_list, hb_s_list, hc_p_list, hc_s_list = [], [], [], [], []
    for l in range(DEPTH):
        lw = (norm_g[l], ada_w[l], ada_b[l], w_in[l], ln_a_g[l], ln_a_b[l], w_spatial[l], b_spatial[l],
              conv_b_w[l], conv_c_w[l], conv_c_b[l], ln_c_g[l], ln_c_b[l], w_branch[l], w_out[l],
              peer_w_query[l], peer_sub_keys[l], peer_u[l], peer_v[l])
        xp, _, hb_p, hc_p = trunk_layer(xp, c_prompt, zero_b, zero_c, *lw)
        xs, v_s, hb_s, hc_s = trunk_layer(xs, c_sample, state_conv_b[l], state_conv_c[l], *lw)
        v_s_list.append(v_s)
        hb_p_list.append(hb_p)
        hb_s_list.append(hb_s)
        hc_p_list.append(hc_p)
        hc_s_list.append(hc_s)
    y_prompt = rmsnorm(xp, final_g)
    y_sample = rmsnorm(xs, final_g)
    new_chunk_v_sample = jnp.stack(v_s_list)
    new_conv_b_prompt = jnp.stack(hb_p_list)
    new_conv_b_sample = jnp.stack(hb_s_list)
    new_conv_c_prompt = jnp.stack(hc_p_list)
    new_conv_c_sample = jnp.stack(hc_s_list)
    return (y_prompt, y_sample, new_chunk_v_sample, new_conv_b_prompt, new_conv_b_sample,
            new_conv_c_prompt, new_conv_c_sample)
```

```python
import functools

import jax
import jax.numpy as jnp
from jax import lax
from jax.experimental import pallas as pl
from jax.experimental.pallas import tpu as pltpu

F32 = jnp.float32
BF16 = jnp.bfloat16
EPS = 1e-6

VMEM_LIMIT_BYTES = 56 * 1024 * 1024
LANES = 128
SUBLANES = 8
MXU_COLS = 256

PEER_TOPK = 16
CONV_ROWS = 32


def _nt_dot(a, b):
    return lax.dot_general(a, b, (((1,), (1,)), ((), ())), preferred_element_type=F32)


def _dot(a, b):
    return jnp.dot(a, b, preferred_element_type=F32)


def _rms(x, g):
    return x * lax.rsqrt(jnp.mean(x * x, axis=-1, keepdims=True) + EPS) * g


def _ln(x, g, b):
    mu = jnp.mean(x, axis=-1, keepdims=True)
    xc = x - mu
    var = jnp.mean(xc * xc, axis=-1, keepdims=True)
    return xc * lax.rsqrt(var + EPS) * g + b


def _gelu(x):
    return 0.5 * x * (1.0 + jnp.tanh(0.7978845608028654 * (x + 0.044715 * (x * x * x))))


def _sigmoid(x):
    return 1.0 / (1.0 + jnp.exp(-x))


def _const_spec(shape):
    nd = len(shape)
    return pl.BlockSpec(shape, lambda *_: (0,) * nd, pipeline_mode=pl.Buffered(1))


def _mod_kernel(c_ref, w_ref, b_ref, o_ref):
    c = c_ref[...]
    s = (c * _sigmoid(c)).astype(BF16)
    o_ref[0] = _dot(s, w_ref[0].astype(BF16)) + b_ref[0]


def _mod_call(c_all, ada_w, ada_b):
    depth, d, n = ada_w.shape
    m = c_all.shape[0]
    tn = 1536
    return pl.pallas_call(
        _mod_kernel,
        grid=(depth, n // tn),
        in_specs=[
            pl.BlockSpec((m, d), lambda l, j: (0, 0)),
            pl.BlockSpec((1, d, tn), lambda l, j: (l, 0, j)),
            pl.BlockSpec((1, 1, tn), lambda l, j: (l, 0, j)),
        ],
        out_specs=pl.BlockSpec((1, m, tn), lambda l, j: (l, 0, j)),
        out_shape=jax.ShapeDtypeStruct((depth, m, n), F32),
        compiler_params=pltpu.CompilerParams(
            dimension_semantics=("arbitrary", "arbitrary"), vmem_limit_bytes=VMEM_LIMIT_BYTES),
        name="adaln_mod",
    )(c_all, ada_w, ada_b.reshape(depth, 1, n))


def _z_slices(d):
    h = d // 2
    names = ["ua", "va", "hb", "bb", "cb", "ac", "gc"]
    out, lo = {}, 0
    for n in names:
        out[n] = (lo, lo + h)
        lo += h
    for k in range(3):
        out[f"g{k}"] = (lo, lo + d)
        lo += d
    return out


def _token_tail(x, merged, mod, ng_ref, wout_ref, wq_ref, xmid_ref, h2_ref, q_ref, d):
    gate1 = mod[:, 2 * d:3 * d]
    shift2, scale2 = mod[:, 3 * d:4 * d], mod[:, 4 * d:5 * d]
    xm = x + gate1 * _dot(merged.astype(BF16), wout_ref[...])
    h2 = (_rms(xm, ng_ref[1:2, :]) * (1.0 + scale2) + shift2).astype(BF16)
    xmid_ref[0] = xm
    h2_ref[0] = h2
    q_ref[0] = _dot(h2, wq_ref[...]).astype(BF16)


def _k1_prompt_kernel(x_ref, mod_ref, ng_ref, win_ref, lnag_ref, lnab_ref, ws_ref, bs_ref, cbw_ref, ccw_ref,
                      ccb_ref, lncg_ref, lncb_ref, wbr_ref, wout_ref, wq_ref,
                      xmid_ref, h2_ref, q_ref, hb_ref, hc_ref,
                      ci_buf, glu_buf, yc_buf, mix_buf, *, tt, d):
    t = pl.program_id(1)
    dh = d // 2
    zs = _z_slices(d)
    taps_c = ccw_ref.shape[0]

    @pl.when(t == 0)
    def _():
        ci_buf[0:8, :] = jnp.zeros((8, dh), F32)
        glu_buf[0:32, :] = jnp.zeros((32, dh), F32)

    x = x_ref[0]
    mod = mod_ref[0]
    shift1, scale1 = mod[:, 0:d], mod[:, d:2 * d]
    hb16 = (_rms(x, ng_ref[0:1, :]) * (1.0 + scale1) + shift1).astype(BF16)

    def z(name):
        lo, hi = zs[name]
        return _dot(hb16, win_ref[:, lo:hi])

    ci = z("cb") * z("hb")
    ci_buf[8:8 + tt, :] = ci
    conv_b = (cbw_ref[0:1, :] * ci_buf[6:6 + tt, :] + cbw_ref[1:2, :] * ci_buf[7:7 + tt, :]
              + cbw_ref[2:3, :] * ci)
    yb = z("bb") * conv_b
    merged = _sigmoid(z("g1")) * _dot(yb.astype(BF16), wbr_ref[1])

    glu_buf[32:32 + tt, :] = z("ac") * _sigmoid(z("gc"))
    for r in range(tt // CONV_ROWS):
        base = r * CONV_ROWS + 32 - (taps_c - 1)
        acc = ccb_ref[...] + ccw_ref[0:1, :] * glu_buf[base:base + CONV_ROWS, :]
        for k in range(1, taps_c):
            acc = acc + ccw_ref[k:k + 1, :] * glu_buf[base + k:base + k + CONV_ROWS, :]
        y = _ln(acc, lncg_ref[...], lncb_ref[...])
        yc_buf[r * CONV_ROWS:(r + 1) * CONV_ROWS, :] = y * _sigmoid(y)
    merged = merged + _sigmoid(z("g2")) * _dot(yc_buf[...].astype(BF16), wbr_ref[2])

    chunk = ws_ref.shape[1]
    gdim = dh // ws_ref.shape[0]
    va = _ln(_gelu(z("va")), lnag_ref[...], lnab_ref[...]).astype(BF16)
    for c in range(tt // chunk):
        for g in range(ws_ref.shape[0]):
            mix_buf[c * chunk:(c + 1) * chunk, g * gdim:(g + 1) * gdim] = (
                _dot(ws_ref[g], va[c * chunk:(c + 1) * chunk, g * gdim:(g + 1) * gdim])
                + bs_ref[:, g * gdim:(g + 1) * gdim])
    ya = _gelu(z("ua")) * mix_buf[...]
    merged = merged + _sigmoid(z("g0")) * _dot(ya.astype(BF16), wbr_ref[0])

    _token_tail(x, merged, mod, ng_ref, wout_ref, wq_ref, xmid_ref, h2_ref, q_ref, d)

    @pl.when(t == pl.num_programs(1) - 1)
    def _():
        hb_ref[0] = ci_buf[tt + 6:tt + 8, :]
        hc_ref[0] = glu_buf[tt + 2:tt + 32, :]

    ci_buf[0:8, :] = ci_buf[tt:tt + 8, :]
    glu_buf[0:32, :] = glu_buf[tt:tt + 32, :]


def _k1_sample_kernel(x_ref, mod_ref, ng_ref, win_ref, lnag_ref, lnab_ref, w00_ref, b0_ref, cbw_ref, ccw_ref,
                      ccb_ref, lncg_ref, lncb_ref, wbr_ref, wout_ref, wq_ref, hb0_ref, hb1_ref, hc_hist_ref,
                      xmid_ref, h2_ref, q_ref, va_ref, ci_ref, glu_ref, *, d):
    zs = _z_slices(d)
    taps_c = ccw_ref.shape[0]
    x = x_ref[0]
    mod = mod_ref[0]
    shift1, scale1 = mod[:, 0:d], mod[:, d:2 * d]
    hb16 = (_rms(x, ng_ref[0:1, :]) * (1.0 + scale1) + shift1).astype(BF16)

    def z(name):
        lo, hi = zs[name]
        return _dot(hb16, win_ref[:, lo:hi])

    ci = z("cb") * z("hb")
    ci_ref[...] = ci
    conv_b = cbw_ref[0:1, :] * hb0_ref[...] + cbw_ref[1:2, :] * hb1_ref[...] + cbw_ref[2:3, :] * ci
    yb = z("bb") * conv_b
    merged = _sigmoid(z("g1")) * _dot(yb.astype(BF16), wbr_ref[1])

    glu = z("ac") * _sigmoid(z("gc"))
    glu_ref[...] = glu
    acc = ccb_ref[...] + ccw_ref[taps_c - 1:taps_c, :] * glu
    for k in range(taps_c - 1):
        acc = acc + ccw_ref[k:k + 1, :] * hc_hist_ref[k]
    y = _ln(acc, lncg_ref[...], lncb_ref[...])
    yc = y * _sigmoid(y)
    merged = merged + _sigmoid(z("g2")) * _dot(yc.astype(BF16), wbr_ref[2])

    va = _ln(_gelu(z("va")), lnag_ref[...], lnab_ref[...])
    va_ref[...] = va
    mixed = w00_ref[...].astype(F32) * va.astype(BF16).astype(F32) + b0_ref[...]
    ya = _gelu(z("ua")) * mixed
    merged = merged + _sigmoid(z("g0")) * _dot(ya.astype(BF16), wbr_ref[0])

    _token_tail(x, merged, mod, ng_ref, wout_ref, wq_ref, xmid_ref, h2_ref, q_ref, d)


def _k1_prompt_call(x, mod, lw, *, tt):
    b, t, d = x.shape
    dh = d // 2
    dq = lw["wq"].shape[1]
    kern = functools.partial(_k1_prompt_kernel, tt=tt, d=d)
    consts = [lw["ng"], lw["win"], lw["lnag"], lw["lnab"], lw["ws"], lw["bs"], lw["cbw"], lw["ccw"], lw["ccb"],
              lw["lncg"], lw["lncb"], lw["wbr"], lw["wout"], lw["wq"]]
    return pl.pallas_call(
        kern,
        grid=(b, t // tt),
        in_specs=[pl.BlockSpec((1, tt, d), lambda i, j: (i, j, 0)),
                  pl.BlockSpec((1, 1, mod.shape[-1]), lambda i, j: (i, 0, 0))]
        + [_const_spec(c.shape) for c in consts],
        out_specs=[pl.BlockSpec((1, tt, d), lambda i, j: (i, j, 0)),
                   pl.BlockSpec((1, tt, d), lambda i, j: (i, j, 0)),
                   pl.BlockSpec((1, tt, dq), lambda i, j: (i, j, 0)),
                   pl.BlockSpec((1, 2, dh), lambda i, j: (i, 0, 0)),
                   pl.BlockSpec((1, 30, dh), lambda i, j: (i, 0, 0))],
        out_shape=[jax.ShapeDtypeStruct((b, t, d), F32),
                   jax.ShapeDtypeStruct((b, t, d), BF16),
                   jax.ShapeDtypeStruct((b, t, dq), BF16),
                   jax.ShapeDtypeStruct((b, 2, dh), F32),
                   jax.ShapeDtypeStruct((b, 30, dh), F32)],
        scratch_shapes=[pltpu.VMEM((tt + 8, dh), F32), pltpu.VMEM((tt + 32, dh), F32),
                        pltpu.VMEM((tt, dh), F32), pltpu.VMEM((tt, dh), F32)],
        compiler_params=pltpu.CompilerParams(
            dimension_semantics=("arbitrary", "arbitrary"), vmem_limit_bytes=VMEM_LIMIT_BYTES),
        name="token_prompt",
    )(x, mod.reshape(b, 1, -1), *consts)


def _k1_sample_call(x, mod, lw, hb0, hb1, hc_hist):
    n, d = x.shape
    dh = d // 2
    dq = lw["wq"].shape[1]
    kern = functools.partial(_k1_sample_kernel, d=d)
    consts = [lw["ng"], lw["win"], lw["lnag"], lw["lnab"], lw["w00"], lw["b0"], lw["cbw"], lw["ccw"], lw["ccb"],
              lw["lncg"], lw["lncb"], lw["wbr"], lw["wout"], lw["wq"], hb0, hb1, hc_hist]
    row = lambda w: pl.BlockSpec((n, w), lambda i: (0, 0))
    return pl.pallas_call(
        kern,
        grid=(1,),
        in_specs=[pl.BlockSpec((1, n, d), lambda i: (0, 0, 0)),
                  pl.BlockSpec((1, n, mod.shape[-1]), lambda i: (0, 0, 0))]
        + [_const_spec(c.shape) for c in consts],
        out_specs=[pl.BlockSpec((1, n, d), lambda i: (0, 0, 0)),
                   pl.BlockSpec((1, n, d), lambda i: (0, 0, 0)),
                   pl.BlockSpec((1, n, dq), lambda i: (0, 0, 0)),
                   row(dh), row(dh), row(dh)],
        out_shape=[jax.ShapeDtypeStruct((1, n, d), F32),
                   jax.ShapeDtypeStruct((1, n, d), BF16),
                   jax.ShapeDtypeStruct((1, n, dq), BF16),
                   jax.ShapeDtypeStruct((n, dh), F32),
                   jax.ShapeDtypeStruct((n, dh), F32),
                   jax.ShapeDtypeStruct((n, dh), F32)],
        compiler_params=pltpu.CompilerParams(
            dimension_semantics=("arbitrary",), vmem_limit_bytes=VMEM_LIMIT_BYTES),
        name="token_sample",
    )(x[None], mod[None], *consts)


def _peer_kernel(h2_ref, q_ref, x_ref, gate_ref, keys_ref, u_ref, vt_ref, fg_ref, out_ref,
                 s1_s, s2_s, e1_s, e2_s, tau_s, top_s, a_s, wa_s, acc_s, *, ch, final):
    e = pl.program_id(1)
    n_heads, n_chunks, n_keys, _ = s1_s.shape
    rows_per_block = u_ref.shape[0] // n_keys
    neg_inf = jnp.float32(-jnp.inf)

    @pl.when(e == 0)
    def _scores_and_thresholds():
        for c in range(n_chunks):
            qc = q_ref[c * ch:(c + 1) * ch, :]
            for h in range(n_heads):
                for p, s_s in enumerate((s1_s, s2_s)):
                    col = (2 * h + p) * n_keys
                    s = _nt_dot(keys_ref[h, p], qc[:, col:col + n_keys])
                    s_s[h, c] = s
                    vals = s
                    for r in range(PEER_TOPK):
                        m = jnp.max(vals, axis=0, keepdims=True)
                        top_s[p, r, h:h + 1, :] = m
                        vals = jnp.where(vals == m, neg_inf, vals)
            a = [top_s[0, r] for r in range(PEER_TOPK)]
            b = [top_s[1, r] for r in range(PEER_TOPK)]
            cands = [a[r] + b[s] for r in range(PEER_TOPK) for s in range(PEER_TOPK // (r + 1))]
            vals = list(cands)
            tau = None
            for r in range(PEER_TOPK):
                m = functools.reduce(jnp.maximum, vals)
                if r == PEER_TOPK - 1:
                    tau = m
                else:
                    vals = [jnp.where(v == m, neg_inf, v) for v in vals]
            top = cands[0]
            zsum = functools.reduce(
                lambda acc, v: acc + jnp.where(v >= tau, jnp.exp(v - top), 0.0), cands, jnp.zeros_like(tau))
            zinv = 1.0 / zsum
            tau_s[c] = tau
            for h in range(n_heads):
                e1_s[h, c] = jnp.exp(s1_s[h, c] - a[0][h:h + 1, :])
                e2_s[h, c] = jnp.exp(s2_s[h, c] - b[0][h:h + 1, :]) * zinv[h:h + 1, :]

    for c in range(n_chunks):
        a_s[...] = _nt_dot(u_ref[...], h2_ref[c * ch:(c + 1) * ch, :])

        def row_body(il, carry):
            i = e * rows_per_block + il
            r0 = pl.multiple_of(il * n_keys, n_keys)
            i8 = pl.multiple_of((i // SUBLANES) * SUBLANES, SUBLANES)
            to_top = (SUBLANES - i % SUBLANES) % SUBLANES

            def row_i(ref, h, ls):
                return pltpu.roll(ref[h, c, pl.ds(i8, SUBLANES), ls], to_top, 0)[0:1, :]

            for lt in range(ch // LANES):
                ls = slice(lt * LANES, (lt + 1) * LANES)
                wsum = jnp.zeros((n_keys, LANES), F32)
                for h in range(n_heads):
                    ssum = row_i(s1_s, h, ls) + s2_s[h, c, :, ls]
                    w = row_i(e1_s, h, ls) * e2_s[h, c, :, ls]
                    wsum = wsum + jnp.where(ssum >= tau_s[c, h:h + 1, ls], w, 0.0)
                wa_s[pl.ds(r0, n_keys), ls] = (wsum * _gelu(a_s[pl.ds(r0, n_keys), ls])).astype(BF16)
            return carry

        lax.fori_loop(0, rows_per_block, row_body, 0)
        part = _dot(vt_ref[...], wa_s[...])

        @pl.when(e == 0)
        def _():
            acc_s[c] = part

        @pl.when(e != 0)
        def _():
            acc_s[c] = acc_s[c] + part

    @pl.when(e == pl.num_programs(1) - 1)
    def _epilogue():
        for c in range(n_chunks):
            rows = slice(c * ch, (c + 1) * ch)
            y = x_ref[rows, :] + gate_ref[0] * acc_s[c].T
            if final:
                y = _rms(y, fg_ref[...])
            out_ref[rows, :] = y


def _peer_call(h2, q, x, gate, keys, u, vt, fg, *, tile, eb, final):
    n, d = x.shape
    n_heads, _, n_keys, _ = keys.shape
    n_exp = u.shape[0]
    ch = min(MXU_COLS, tile)
    n_chunks = tile // ch
    groups, grows, _ = gate.shape
    tiles_per_group = (n // tile) // groups
    kern = functools.partial(_peer_kernel, ch=ch, final=final)
    score_scratch = pltpu.VMEM((n_heads, n_chunks, n_keys, ch), F32)
    return pl.pallas_call(
        kern,
        grid=(n // tile, n_exp // eb),
        in_specs=[pl.BlockSpec((tile, d), lambda t, e: (t, 0)),
                  pl.BlockSpec((tile, q.shape[1]), lambda t, e: (t, 0)),
                  pl.BlockSpec((tile, d), lambda t, e: (t, 0)),
                  pl.BlockSpec((1, grows, d), lambda t, e: (t // tiles_per_group, 0, 0)),
                  _const_spec(keys.shape),
                  pl.BlockSpec((eb, d), lambda t, e: (e, 0)),
                  pl.BlockSpec((d, eb), lambda t, e: (0, e)),
                  _const_spec(fg.shape)],
        out_specs=pl.BlockSpec((tile, d), lambda t, e: (t, 0)),
        out_shape=jax.ShapeDtypeStruct((n, d), F32),
        scratch_shapes=[score_scratch, score_scratch, score_scratch, score_scratch,
                        pltpu.VMEM((n_chunks, n_heads, ch), F32),
                        pltpu.VMEM((2, PEER_TOPK, n_heads, ch), F32),
                        pltpu.VMEM((eb, ch), F32),
                        pltpu.VMEM((eb, ch), BF16),
                        pltpu.VMEM((n_chunks, d, ch), F32)],
        compiler_params=pltpu.CompilerParams(
            dimension_semantics=("arbitrary", "arbitrary"), vmem_limit_bytes=VMEM_LIMIT_BYTES),
        name="peer_final" if final else "peer",
    )(h2, q, x, gate, keys, u, vt, fg)


TOKEN_TILE = 256
PEER_TILE = 512
PEER_EXPERT_BLOCK = 2048


def kernel(x_prompt, x_sample, c_prompt, c_sample, state_conv_b, state_conv_c, norm_g, final_g, ada_w, ada_b, w_in, ln_a_g, ln_a_b, w_spatial, b_spatial, conv_b_w, conv_c_w, conv_c_b, ln_c_g, ln_c_b, w_branch, w_out, peer_w_query, peer_sub_keys, peer_u, peer_v):
    depth = w_in.shape[0]
    bp, seq, d = x_prompt.shape
    ns = x_sample.shape[0]
    dh = d // 2
    groups, chunk = w_spatial.shape[1], w_spatial.shape[2]
    gdim = dh // groups

    mod_all = _mod_call(jnp.concatenate([c_prompt, c_sample], axis=0), ada_w, ada_b)

    tril = jnp.tril(jnp.ones((chunk, chunk), dtype=bool))
    ws_all = jnp.where(tril[None, None], w_spatial, 0.0).astype(BF16)
    bs_all = jnp.repeat(jnp.swapaxes(b_spatial, 1, 2), gdim, axis=2)
    w00_all = jnp.repeat(w_spatial[:, :, 0, 0], gdim, axis=1).astype(BF16)[:, None, :]
    b0_all = jnp.repeat(b_spatial[:, :, 0], gdim, axis=1)[:, None, :]
    win_all = w_in.astype(BF16)
    wbr_all = w_branch.astype(BF16)
    wout_all = w_out.astype(BF16)
    wq_all = peer_w_query.astype(BF16)
    keys_all = peer_sub_keys.astype(BF16)
    u_all = peer_u.astype(BF16)
    vt_all = jnp.swapaxes(peer_v, 1, 2).astype(BF16)
    hc_hist_all = jnp.swapaxes(state_conv_c, 1, 2)
    fg = final_g[None, :]

    xp = x_prompt
    xs = x_sample.reshape(ns, d)
    v_s, hb_p, hb_s, hc_p, hc_s = [], [], [], [], []
    for l in range(depth):
        lw = dict(ng=norm_g[l], win=win_all[l], lnag=ln_a_g[l][None], lnab=ln_a_b[l][None], ws=ws_all[l],
                  bs=bs_all[l], w00=w00_all[l], b0=b0_all[l], cbw=conv_b_w[l], ccw=conv_c_w[l],
                  ccb=conv_c_b[l][None], lncg=ln_c_g[l][None], lncb=ln_c_b[l][None], wbr=wbr_all[l],
                  wout=wout_all[l], wq=wq_all[l])
        mod_p, mod_s = mod_all[l, :bp], mod_all[l, bp:]
        final = l == depth - 1

        xmid, h2, q, hb, hc = _k1_prompt_call(xp, mod_p, lw, tt=TOKEN_TILE)
        hb_p.append(hb)
        hc_p.append(hc)
        xp = _peer_call(h2.reshape(bp * seq, d), q.reshape(bp * seq, -1), xmid.reshape(bp * seq, d),
                        mod_p[:, None, 5 * d:], keys_all[l], u_all[l], vt_all[l], fg,
                        tile=PEER_TILE, eb=PEER_EXPERT_BLOCK, final=final).reshape(bp, seq, d)

        xmid, h2, q, va, ci, glu = _k1_sample_call(xs, mod_s, lw, state_conv_b[l, :, 0], state_conv_b[l, :, 1],
                                                   hc_hist_all[l])
        v_s.append(va[:, None, :])
        hb_s.append(jnp.stack([state_conv_b[l, :, 1], ci], axis=1))
        hc_s.append(jnp.concatenate([state_conv_c[l, :, 1:], glu[:, None, :]], axis=1))
        xs = _peer_call(h2[0], q[0], xmid[0], mod_s[None, :, 5 * d:], keys_all[l], u_all[l], vt_all[l], fg,
                        tile=ns, eb=PEER_EXPERT_BLOCK, final=final)

    return (xp, xs.reshape(ns, 1, d), jnp.stack(v_s), jnp.stack(hb_p), jnp.stack(hb_s), jnp.stack(hc_p),
            jnp.stack(hc_s))
```

```python
import functools

import jax
import jax.numpy as jnp
from jax import lax
from jax.experimental import pallas as pl
from jax.experimental.pallas import tpu as pltpu

F32 = jnp.float32
BF16 = jnp.bfloat16
EPS = 1e-6

VMEM_LIMIT_BYTES = 56 * 1024 * 1024
LANES = 128
SUBLANES = 8
MXU_COLS = 256

PEER_TOPK = 16
CONV_ROWS = 32


def _nt_dot(a, b):
    return lax.dot_general(a, b, (((1,), (1,)), ((), ())), preferred_element_type=F32)


def _dot(a, b):
    return jnp.dot(a, b, preferred_element_type=F32)


def _rms(x, g):
    return x * lax.rsqrt(jnp.mean(x * x, axis=-1, keepdims=True) + EPS) * g


def _ln(x, g, b):
    mu = jnp.mean(x, axis=-1, keepdims=True)
    xc = x - mu
    var = jnp.mean(xc * xc, axis=-1, keepdims=True)
    return xc * lax.rsqrt(var + EPS) * g + b


def _gelu(x):
    return 0.5 * x * (1.0 + jnp.tanh(0.7978845608028654 * (x + 0.044715 * (x * x * x))))


def _sigmoid(x):
    return 1.0 / (1.0 + jnp.exp(-x))


def _const_spec(shape):
    nd = len(shape)
    return pl.BlockSpec(shape, lambda *_: (0,) * nd, pipeline_mode=pl.Buffered(1))


def _mod_kernel(c_ref, w_ref, b_ref, o_ref):
    c = c_ref[...]
    s = (c * _sigmoid(c)).astype(BF16)
    o_ref[0] = _dot(s, w_ref[0].astype(BF16)) + b_ref[0]


def _mod_call(c_all, ada_w, ada_b):
    depth, d, n = ada_w.shape
    m = c_all.shape[0]
    tn = 1536
    return pl.pallas_call(
        _mod_kernel,
        grid=(depth, n // tn),
        in_specs=[
            pl.BlockSpec((m, d), lambda l, j: (0, 0)),
            pl.BlockSpec((1, d, tn), lambda l, j: (l, 0, j)),
            pl.BlockSpec((1, 1, tn), lambda l, j: (l, 0, j)),
        ],
        out_specs=pl.BlockSpec((1, m, tn), lambda l, j: (l, 0, j)),
        out_shape=jax.ShapeDtypeStruct((depth, m, n), F32),
        compiler_params=pltpu.CompilerParams(
            dimension_semantics=("arbitrary", "arbitrary"), vmem_limit_bytes=VMEM_LIMIT_BYTES),
        name="adaln_mod",
    )(c_all, ada_w, ada_b.reshape(depth, 1, n))


def _z_slices(d):
    h = d // 2
    names = ["ua", "va", "hb", "bb", "cb", "ac", "gc"]
    out, lo = {}, 0
    for n in names:
        out[n] = (lo, lo + h)
        lo += h
    for k in range(3):
        out[f"g{k}"] = (lo, lo + d)
        lo += d
    return out


def _token_tail(x, merged, mod, ng_ref, wout_ref, wq_ref, xmid_ref, h2_ref, q_ref, d):
    gate1 = mod[:, 2 * d:3 * d]
    shift2, scale2 = mod[:, 3 * d:4 * d], mod[:, 4 * d:5 * d]
    xm = x + gate1 * _dot(merged.astype(BF16), wout_ref[...])
    h2 = (_rms(xm, ng_ref[1:2, :]) * (1.0 + scale2) + shift2).astype(BF16)
    xmid_ref[0] = xm
    h2_ref[0] = h2
    q_ref[0] = _dot(h2, wq_ref[...]).astype(BF16)


def _k1_prompt_kernel(x_ref, mod_ref, ng_ref, win_ref, lnag_ref, lnab_ref, ws_ref, bs_ref, cbw_ref, ccw_ref,
                      ccb_ref, lncg_ref, lncb_ref, wbr_ref, wout_ref, wq_ref,
                      xmid_ref, h2_ref, q_ref, hb_ref, hc_ref,
                      ci_buf, glu_buf, yc_buf, mix_buf, *, tt, d):
    t = pl.program_id(1)
    dh = d // 2
    zs = _z_slices(d)
    taps_c = ccw_ref.shape[0]

    @pl.when(t == 0)
    def _():
        ci_buf[0:8, :] = jnp.zeros((8, dh), F32)
        glu_buf[0:32, :] = jnp.zeros((32, dh), F32)

    x = x_ref[0]
    mod = mod_ref[0]
    shift1, scale1 = mod[:, 0:d], mod[:, d:2 * d]
    hb16 = (_rms(x, ng_ref[0:1, :]) * (1.0 + scale1) + shift1).astype(BF16)

    def z(name):
        lo, hi = zs[name]
        return _dot(hb16, win_ref[:, lo:hi])

    ci = z("cb") * z("hb")
    ci_buf[8:8 + tt, :] = ci
    conv_b = (cbw_ref[0:1, :] * ci_buf[6:6 + tt, :] + cbw_ref[1:2, :] * ci_buf[7:7 + tt, :]
              + cbw_ref[2:3, :] * ci)
    yb = z("bb") * conv_b
    merged = _sigmoid(z("g1")) * _dot(yb.astype(BF16), wbr_ref[1])

    glu_buf[32:32 + tt, :] = z("ac") * _sigmoid(z("gc"))
    for r in range(tt // CONV_ROWS):
        base = r * CONV_ROWS + 32 - (taps_c - 1)
        acc = ccb_ref[...] + ccw_ref[0:1, :] * glu_buf[base:base + CONV_ROWS, :]
        for k in range(1, taps_c):
            acc = acc + ccw_ref[k:k + 1, :] * glu_buf[base + k:base + k + CONV_ROWS, :]
        y = _ln(acc, lncg_ref[...], lncb_ref[...])
        yc_buf[r * CONV_ROWS:(r + 1) * CONV_ROWS, :] = y * _sigmoid(y)
    merged = merged + _sigmoid(z("g2")) * _dot(yc_buf[...].astype(BF16), wbr_ref[2])

    chunk = ws_ref.shape[1]
    gdim = dh // ws_ref.shape[0]
    va = _ln(_gelu(z("va")), lnag_ref[...], lnab_ref[...]).astype(BF16)
    for c in range(tt // chunk):
        for g in range(ws_ref.shape[0]):
            mix_buf[c * chunk:(c + 1) * chunk, g * gdim:(g + 1) * gdim] = (
                _dot(ws_ref[g], va[c * chunk:(c + 1) * chunk, g * gdim:(g + 1) * gdim])
                + bs_ref[:, g * gdim:(g + 1) * gdim])
    ya = _gelu(z("ua")) * mix_buf[...]
    merged = merged + _sigmoid(z("g0")) * _dot(ya.astype(BF16), wbr_ref[0])

    _token_tail(x, merged, mod, ng_ref, wout_ref, wq_ref, xmid_ref, h2_ref, q_ref, d)

    @pl.when(t == pl.num_programs(1) - 1)
    def _():
        hb_ref[0] = ci_buf[tt + 6:tt + 8, :]
        hc_ref[0] = glu_buf[tt + 2:tt + 32, :]

    ci_buf[0:8, :] = ci_buf[tt:tt + 8, :]
    glu_buf[0:32, :] = glu_buf[tt:tt + 32, :]


def _k1_sample_kernel(x_ref, mod_ref, ng_ref, win_ref, lnag_ref, lnab_ref, w00_ref, b0_ref, cbw_ref, ccw_ref,
                      ccb_ref, lncg_ref, lncb_ref, wbr_ref, wout_ref, wq_ref, hb0_ref, hb1_ref, hc_hist_ref,
                      xmid_ref, h2_ref, q_ref, va_ref, ci_ref, glu_ref, *, d):
    zs = _z_slices(d)
    taps_c = ccw_ref.shape[0]
    x = x_ref[0]
    mod = mod_ref[0]
    shift1, scale1 = mod[:, 0:d], mod[:, d:2 * d]
    hb16 = (_rms(x, ng_ref[0:1, :]) * (1.0 + scale1) + shift1).astype(BF16)

    def z(name):
        lo, hi = zs[name]
        return _dot(hb16, win_ref[:, lo:hi])

    ci = z("cb") * z("hb")
    ci_ref[...] = ci
    conv_b = cbw_ref[0:1, :] * hb0_ref[...] + cbw_ref[1:2, :] * hb1_ref[...] + cbw_ref[2:3, :] * ci
    yb = z("bb") * conv_b
    merged = _sigmoid(z("g1")) * _dot(yb.astype(BF16), wbr_ref[1])

    glu = z("ac") * _sigmoid(z("gc"))
    glu_ref[...] = glu
    acc = ccb_ref[...] + ccw_ref[taps_c - 1:taps_c, :] * glu
    for k in range(taps_c - 1):
        acc = acc + ccw_ref[k:k + 1, :] * hc_hist_ref[k]
    y = _ln(acc, lncg_ref[...], lncb_ref[...])
    yc = y * _sigmoid(y)
    merged = merged + _sigmoid(z("g2")) * _dot(yc.astype(BF16), wbr_ref[2])

    va = _ln(_gelu(z("va")), lnag_ref[...], lnab_ref[...])
    va_ref[...] = va
    mixed = w00_ref[...].astype(F32) * va.astype(BF16).astype(F32) + b0_ref[...]
    ya = _gelu(z("ua")) * mixed
    merged = merged + _sigmoid(z("g0")) * _dot(ya.astype(BF16), wbr_ref[0])

    _token_tail(x, merged, mod, ng_ref, wout_ref, wq_ref, xmid_ref, h2_ref, q_ref, d)


def _k1_prompt_call(x, mod, lw, *, tt):
    b, t, d = x.shape
    dh = d // 2
    dq = lw["wq"].shape[1]
    kern = functools.partial(_k1_prompt_kernel, tt=tt, d=d)
    consts = [lw["ng"], lw["win"], lw["lnag"], lw["lnab"], lw["ws"], lw["bs"], lw["cbw"], lw["ccw"], lw["ccb"],
              lw["lncg"], lw["lncb"], lw["wbr"], lw["wout"], lw["wq"]]
    return pl.pallas_call(
        kern,
        grid=(b, t // tt),
        in_specs=[pl.BlockSpec((1, tt, d), lambda i, j: (i, j, 0)),
                  pl.BlockSpec((1, 1, mod.shape[-1]), lambda i, j: (i, 0, 0))]
        + [_const_spec(c.shape) for c in consts],
        out_specs=[pl.BlockSpec((1, tt, d), lambda i, j: (i, j, 0)),
                   pl.BlockSpec((1, tt, d), lambda i, j: (i, j, 0)),
                   pl.BlockSpec((1, tt, dq), lambda i, j: (i, j, 0)),
                   pl.BlockSpec((1, 2, dh), lambda i, j: (i, 0, 0)),
                   pl.BlockSpec((1, 30, dh), lambda i, j: (i, 0, 0))],
        out_shape=[jax.ShapeDtypeStruct((b, t, d), F32),
                   jax.ShapeDtypeStruct((b, t, d), BF16),
                   jax.ShapeDtypeStruct((b, t, dq), BF16),
                   jax.ShapeDtypeStruct((b, 2, dh), F32),
                   jax.ShapeDtypeStruct((b, 30, dh), F32)],
        scratch_shapes=[pltpu.VMEM((tt + 8, dh), F32), pltpu.VMEM((tt + 32, dh), F32),
                        pltpu.VMEM((tt, dh), F32), pltpu.VMEM((tt, dh), F32)],
        compiler_params=pltpu.CompilerParams(
            dimension_semantics=("arbitrary", "arbitrary"), vmem_limit_bytes=VMEM_LIMIT_BYTES),
        name="token_prompt",
    )(x, mod.reshape(b, 1, -1), *consts)


def _k1_sample_call(x, mod, lw, hb0, hb1, hc_hist):
    n, d = x.shape
    dh = d // 2
    dq = lw["wq"].shape[1]
    kern = functools.partial(_k1_sample_kernel, d=d)
    consts = [lw["ng"], lw["win"], lw["lnag"], lw["lnab"], lw["w00"], lw["b0"], lw["cbw"], lw["ccw"], lw["ccb"],
              lw["lncg"], lw["lncb"], lw["wbr"], lw["wout"], lw["wq"], hb0, hb1, hc_hist]
    row = lambda w: pl.BlockSpec((n, w), lambda i: (0, 0))
    return pl.pallas_call(
        kern,
        grid=(1,),
        in_specs=[pl.BlockSpec((1, n, d), lambda i: (0, 0, 0)),
                  pl.BlockSpec((1, n, mod.shape[-1]), lambda i: (0, 0, 0))]
        + [_const_spec(c.shape) for c in consts],
        out_specs=[pl.BlockSpec((1, n, d), lambda i: (0, 0, 0)),
                   pl.BlockSpec((1, n, d), lambda i: (0, 0, 0)),
                   pl.BlockSpec((1, n, dq), lambda i: (0, 0, 0)),
                   row(dh), row(dh), row(dh)],
        out_shape=[jax.ShapeDtypeStruct((1, n, d), F32),
                   jax.ShapeDtypeStruct((1, n, d), BF16),
                   jax.ShapeDtypeStruct((1, n, dq), BF16),
                   jax.ShapeDtypeStruct((n, dh), F32),
                   jax.ShapeDtypeStruct((n, dh), F32),
                   jax.ShapeDtypeStruct((n, dh), F32)],
        compiler_params=pltpu.CompilerParams(
            dimension_semantics=("arbitrary",), vmem_limit_bytes=VMEM_LIMIT_BYTES),
        name="token_sample",
    )(x[None], mod[None], *consts)


GELU_C0 = 0.7978845608028654
GELU_C1 = GELU_C0 * 0.044715


def _aligned(x, m):
    return x if isinstance(x, int) else pl.multiple_of(x, m)


def _peer_kernel(h2_ref, q_ref, x_ref, gate_ref, keys_ref, u_ref, vt_ref, fg_ref, out_ref,
                 th_s, s2_s, e1_s, e2_s, top_s, rowb_s, a0_s, a1_s, wa_s, acc_s, *, ch, final):
    e = pl.program_id(1)
    n_heads, n_chunks, n_keys, _ = th_s.shape
    unit = a0_s.shape[0] * SUBLANES
    rows = unit // n_keys
    rows_per_block = u_ref.shape[0] // n_keys
    upc = u_ref.shape[0] // unit
    n_units = n_chunks * upc
    n_pairs = n_units // 2
    assert upc % 2 == 0 and SUBLANES % rows == 0
    tiles_per_row = n_keys // SUBLANES
    n_top = PEER_TOPK + 1
    neg_inf = jnp.float32(-jnp.inf)

    @pl.when(e == 0)
    def _scores_and_thresholds():
        acc_s[...] = jnp.zeros(acc_s.shape, F32)
        for c in range(n_chunks):
            qc = q_ref[c * ch:(c + 1) * ch, :]
            for h in range(n_heads):
                for p in range(2):
                    col = (2 * h + p) * n_keys
                    s = _nt_dot(keys_ref[h, p], qc[:, col:col + n_keys])
                    if p == 0:
                        th_s[h, c] = s
                    else:
                        s2_s[h, c] = s.reshape(tiles_per_row, SUBLANES, ch)
                    vals = s
                    for r in range(n_top):
                        m = jnp.max(vals, axis=0, keepdims=True)
                        top_s[p, r, h:h + 1, :] = m
                        if r + 1 < n_top:
                            vals = jnp.where(vals == m, neg_inf, vals)
            a = [top_s[0, r] for r in range(n_top)]
            b = [top_s[1, r] for r in range(n_top)]
            cands = [a[r] + b[s] for r in range(n_top) for s in range(n_top // (r + 1))]
            vals = list(cands)
            ranked = []
            for r in range(n_top):
                m = functools.reduce(jnp.maximum, vals)
                ranked.append(m)
                if r + 1 < n_top:
                    vals = [jnp.where(v == m, neg_inf, v) for v in vals]
            tau = 0.5 * (ranked[PEER_TOPK - 1] + ranked[PEER_TOPK])
            top = cands[0]
            zsum = functools.reduce(
                lambda acc, v: acc + jnp.where(v >= tau, jnp.exp(v - top), 0.0), cands, jnp.zeros_like(tau))
            half_zinv = 0.5 / zsum
            for h in range(n_heads):
                s1 = th_s[h, c]
                e1_s[h, c] = jnp.exp(s1 - a[0][h:h + 1, :])
                th_s[h, c] = tau[h:h + 1, :] - s1
                e2 = jnp.exp(s2_s[h, c] - b[0][h:h + 1, :]) * half_zinv[h:h + 1, :]
                e2_s[h, c] = e2

    def unit_pos(n):
        if isinstance(n, int):
            return n // upc, n % upc
        c = lax.div(n, jnp.int32(upc))
        return c, n - c * upc

    def stage_a(n, a_s):
        c, k = unit_pos(n)
        prod = _nt_dot(u_ref[pl.ds(_aligned(k * unit, unit), unit), :],
                       h2_ref[pl.ds(_aligned(c * ch, ch), ch), :])
        a_s[...] = prod.reshape(unit // SUBLANES, SUBLANES, ch)

    def stage_b(n, a_s, slot, half):
        c, k = unit_pos(n)
        i0 = e * rows_per_block + k * rows
        i8 = pl.multiple_of(lax.shift_right_logical(i0, 3) * SUBLANES, SUBLANES)
        to_top = (SUBLANES - (i0 & (SUBLANES - 1))) & (SUBLANES - 1)
        for h in range(n_heads):
            for which, ref in enumerate((th_s, e1_s)):
                tile = pltpu.roll(ref[h, c, pl.ds(i8, SUBLANES), :], to_top, 0)
                for r in range(rows):
                    rowb_s[which, r, h] = jnp.broadcast_to(tile[r:r + 1, :], (SUBLANES, ch))
        for lt in range(ch // LANES):
            ls = slice(lt * LANES, (lt + 1) * LANES)
            for jq in range(tiles_per_row // PEER_J_TILES):
                js = slice(jq * PEER_J_TILES, (jq + 1) * PEER_J_TILES)
                wsum = [None] * rows
                for h in range(n_heads):
                    s2t = s2_s[h, c, js, :, ls]
                    e2t = e2_s[h, c, js, :, ls]
                    for r in range(rows):
                        keep = s2t >= rowb_s[0, r, h, :, ls]
                        term = rowb_s[1, r, h, :, ls] * jnp.where(keep, e2t, 0.0)
                        wsum[r] = term if wsum[r] is None else wsum[r] + term
                for r in range(rows):
                    t0 = r * tiles_per_row + jq * PEER_J_TILES
                    av = a_s[t0:t0 + PEER_J_TILES, :, ls]
                    y = av * (1.0 + jnp.tanh(av * (GELU_C0 + GELU_C1 * (av * av))))
                    r0 = half * unit + t0 * SUBLANES
                    wa_s[slot, r0:r0 + PEER_J_TILES * SUBLANES, ls] = (
                        (wsum[r] * y).reshape(PEER_J_TILES * SUBLANES, LANES).astype(BF16))

    def stage_c(pair, slot):
        c, kp = (pair // (upc // 2), pair % (upc // 2)) if isinstance(pair, int) else (
            lax.div(pair, jnp.int32(upc // 2)), lax.rem(pair, jnp.int32(upc // 2)))
        acc_s[c] = acc_s[c] + _dot(vt_ref[kp], wa_s[slot])

    stage_a(0, a0_s)
    wa_s[1] = jnp.zeros(wa_s.shape[1:], BF16)

    def pair_body(m, carry):
        slot = lax.rem(m, 2)
        stage_a(2 * m + 1, a1_s)
        stage_c(jnp.maximum(m - 1, 0), 1 - slot)
        stage_b(2 * m, a0_s, slot, 0)
        stage_a(jnp.minimum(2 * m + 2, n_units - 1), a0_s)
        stage_b(2 * m + 1, a1_s, slot, 1)
        return carry

    lax.fori_loop(0, n_pairs, pair_body, 0)
    stage_c(n_pairs - 1, (n_pairs - 1) % 2)

    @pl.when(e == pl.num_programs(1) - 1)
    def _epilogue():
        for c in range(n_chunks):
            tok = slice(c * ch, (c + 1) * ch)
            y = x_ref[tok, :] + gate_ref[0] * acc_s[c].T
            if final:
                y = _rms(y, fg_ref[...])
            out_ref[tok, :] = y


def _peer_call(h2, q, x, gate, keys, u, vt, fg, *, tile, eb, final):
    n, d = x.shape
    n_heads, _, n_keys, _ = keys.shape
    n_exp = u.shape[0]
    pair = vt.shape[2]
    unit = pair // 2
    ch = min(MXU_COLS, tile)
    n_chunks = tile // ch
    groups, grows, _ = gate.shape
    tiles_per_group = (n // tile) // groups
    kern = functools.partial(_peer_kernel, ch=ch, final=final)
    row_scratch = pltpu.VMEM((n_heads, n_chunks, n_keys, ch), F32)
    tile_scratch = pltpu.VMEM((n_heads, n_chunks, n_keys // SUBLANES, SUBLANES, ch), F32)
    prod_scratch = pltpu.VMEM((unit // SUBLANES, SUBLANES, ch), F32)
    return pl.pallas_call(
        kern,
        grid=(n // tile, n_exp // eb),
        in_specs=[pl.BlockSpec((tile, d), lambda t, e: (t, 0)),
                  pl.BlockSpec((tile, q.shape[1]), lambda t, e: (t, 0)),
                  pl.BlockSpec((tile, d), lambda t, e: (t, 0)),
                  pl.BlockSpec((1, grows, d), lambda t, e: (t // tiles_per_group, 0, 0)),
                  _const_spec(keys.shape),
                  pl.BlockSpec((eb, d), lambda t, e: (e, 0)),
                  pl.BlockSpec((eb // pair, d, pair), lambda t, e: (e, 0, 0)),
                  _const_spec(fg.shape)],
        out_specs=pl.BlockSpec((tile, d), lambda t, e: (t, 0)),
        out_shape=jax.ShapeDtypeStruct((n, d), F32),
        scratch_shapes=[row_scratch, tile_scratch, row_scratch, tile_scratch,
                        pltpu.VMEM((2, PEER_TOPK + 1, n_heads, ch), F32),
                        pltpu.VMEM((2, unit // n_keys, n_heads, SUBLANES, ch), F32),
                        prod_scratch, prod_scratch,
                        pltpu.VMEM((2, pair, ch), BF16),
                        pltpu.VMEM((n_chunks, d, ch), F32)],
        compiler_params=pltpu.CompilerParams(
            dimension_semantics=("arbitrary", "arbitrary"), vmem_limit_bytes=VMEM_LIMIT_BYTES),
        name="peer_final" if final else "peer",
    )(h2, q, x, gate, keys, u, vt, fg)


TOKEN_TILE = 256
PEER_TILE = 512
PEER_EXPERT_BLOCK = 2048
PEER_UNIT_ROWS = 4
PEER_J_TILES = 4

def kernel(x_prompt, x_sample, c_prompt, c_sample, state_conv_b, state_conv_c, norm_g, final_g, ada_w, ada_b, w_in, ln_a_g, ln_a_b, w_spatial, b_spatial, conv_b_w, conv_c_w, conv_c_b, ln_c_g, ln_c_b, w_branch, w_out, peer_w_query, peer_sub_keys, peer_u, peer_v):
    depth = w_in.shape[0]
    bp, seq, d = x_prompt.shape
    ns = x_sample.shape[0]
    dh = d // 2
    groups, chunk = w_spatial.shape[1], w_spatial.shape[2]
    gdim = dh // groups

    mod_all = _mod_call(jnp.concatenate([c_prompt, c_sample], axis=0), ada_w, ada_b)

    tril = jnp.tril(jnp.ones((chunk, chunk), dtype=bool))
    ws_all = jnp.where(tril[None, None], w_spatial, 0.0).astype(BF16)
    bs_all = jnp.repeat(jnp.swapaxes(b_spatial, 1, 2), gdim, axis=2)
    w00_all = jnp.repeat(w_spatial[:, :, 0, 0], gdim, axis=1).astype(BF16)[:, None, :]
    b0_all = jnp.repeat(b_spatial[:, :, 0], gdim, axis=1)[:, None, :]
    win_all = w_in.astype(BF16)
    wbr_all = w_branch.astype(BF16)
    wout_all = w_out.astype(BF16)
    wq_all = peer_w_query.astype(BF16)
    keys_all = peer_sub_keys.astype(BF16)
    u_all = peer_u.astype(BF16)
    n_exp = peer_u.shape[1]
    pair = 2 * PEER_UNIT_ROWS * peer_sub_keys.shape[3]
    vt_all = jnp.swapaxes(peer_v.reshape(depth, n_exp // pair, pair, d), 2, 3).astype(BF16)
    hc_hist_all = jnp.swapaxes(state_conv_c, 1, 2)
    fg = final_g[None, :]

    xp = x_prompt
    xs = x_sample.reshape(ns, d)
    v_s, hb_p, hb_s, hc_p, hc_s = [], [], [], [], []
    for l in range(depth):
        lw = dict(ng=norm_g[l], win=win_all[l], lnag=ln_a_g[l][None], lnab=ln_a_b[l][None], ws=ws_all[l],
                  bs=bs_all[l], w00=w00_all[l], b0=b0_all[l], cbw=conv_b_w[l], ccw=conv_c_w[l],
                  ccb=conv_c_b[l][None], lncg=ln_c_g[l][None], lncb=ln_c_b[l][None], wbr=wbr_all[l],
                  wout=wout_all[l], wq=wq_all[l])
        mod_p, mod_s = mod_all[l, :bp], mod_all[l, bp:]
        final = l == depth - 1

        xmid, h2, q, hb, hc = _k1_prompt_call(xp, mod_p, lw, tt=TOKEN_TILE)
        hb_p.append(hb)
        hc_p.append(hc)
        xp = _peer_call(h2.reshape(bp * seq, d), q.reshape(bp * seq, -1), xmid.reshape(bp * seq, d),
                        mod_p[:, None, 5 * d:], keys_all[l], u_all[l], vt_all[l], fg,
                        tile=PEER_TILE, eb=PEER_EXPERT_BLOCK, final=final).reshape(bp, seq, d)

        xmid, h2, q, va, ci, glu = _k1_sample_call(xs, mod_s, lw, state_conv_b[l, :, 0], state_conv_b[l, :, 1],
                                                   hc_hist_all[l])
        v_s.append(va[:, None, :])
        hb_s.append(jnp.stack([state_conv_b[l, :, 1], ci], axis=1))
        hc_s.append(jnp.concatenate([state_conv_c[l, :, 1:], glu[:, None, :]], axis=1))
        xs = _peer_call(h2[0], q[0], xmid[0], mod_s[None, :, 5 * d:], keys_all[l], u_all[l], vt_all[l], fg,
                        tile=ns, eb=PEER_EXPERT_BLOCK, final=final)

    return (xp, xs.reshape(ns, 1, d), jnp.stack(v_s), jnp.stack(hb_p), jnp.stack(hb_s), jnp.stack(hc_p),
            jnp.stack(hc_s))
```

```python
import functools

import jax
import jax.numpy as jnp
from jax import lax
from jax.experimental import pallas as pl
from jax.experimental.pallas import tpu as pltpu

F32 = jnp.float32
BF16 = jnp.bfloat16
EPS = 1e-6

VMEM_LIMIT_BYTES = 56 * 1024 * 1024
LANES = 128
SUBLANES = 8
BF16_ROWS = 16
MXU_COLS = 256

PEER_TOPK = 16
CONV_ROWS = 32


def _nt_dot(a, b):
    return lax.dot_general(a, b, (((1,), (1,)), ((), ())), preferred_element_type=F32)


def _dot(a, b):
    return jnp.dot(a, b, preferred_element_type=F32)


def _rms(x, g):
    return x * lax.rsqrt(jnp.mean(x * x, axis=-1, keepdims=True) + EPS) * g


def _ln(x, g, b):
    mu = jnp.mean(x, axis=-1, keepdims=True)
    xc = x - mu
    var = jnp.mean(xc * xc, axis=-1, keepdims=True)
    return xc * lax.rsqrt(var + EPS) * g + b


def _gelu(x):
    return 0.5 * x * (1.0 + jnp.tanh(0.7978845608028654 * (x + 0.044715 * (x * x * x))))


def _sigmoid(x):
    return 1.0 / (1.0 + jnp.exp(-x))


def _const_spec(shape):
    nd = len(shape)
    return pl.BlockSpec(shape, lambda *_: (0,) * nd, pipeline_mode=pl.Buffered(1))


def _mod_kernel(c_ref, w_ref, b_ref, o_ref):
    c = c_ref[...]
    s = (c * _sigmoid(c)).astype(BF16)
    o_ref[0] = _dot(s, w_ref[0].astype(BF16)) + b_ref[0]


def _mod_call(c_all, ada_w, ada_b):
    depth, d, n = ada_w.shape
    m = c_all.shape[0]
    tn = 1536
    return pl.pallas_call(
        _mod_kernel,
        grid=(depth, n // tn),
        in_specs=[
            pl.BlockSpec((m, d), lambda l, j: (0, 0)),
            pl.BlockSpec((1, d, tn), lambda l, j: (l, 0, j)),
            pl.BlockSpec((1, 1, tn), lambda l, j: (l, 0, j)),
        ],
        out_specs=pl.BlockSpec((1, m, tn), lambda l, j: (l, 0, j)),
        out_shape=jax.ShapeDtypeStruct((depth, m, n), F32),
        compiler_params=pltpu.CompilerParams(
            dimension_semantics=("arbitrary", "arbitrary"), vmem_limit_bytes=VMEM_LIMIT_BYTES),
        name="adaln_mod",
    )(c_all, ada_w, ada_b.reshape(depth, 1, n))


def _z_slices(d):
    h = d // 2
    names = ["ua", "va", "hb", "bb", "cb", "ac", "gc"]
    out, lo = {}, 0
    for n in names:
        out[n] = (lo, lo + h)
        lo += h
    for k in range(3):
        out[f"g{k}"] = (lo, lo + d)
        lo += d
    return out


def _token_tail(x, merged, mod, ng_ref, wout_ref, wq_ref, xmid_ref, h2_ref, q_ref, d):
    gate1 = mod[:, 2 * d:3 * d]
    shift2, scale2 = mod[:, 3 * d:4 * d], mod[:, 4 * d:5 * d]
    xm = x + gate1 * _dot(merged.astype(BF16), wout_ref[...])
    h2 = (_rms(xm, ng_ref[1:2, :]) * (1.0 + scale2) + shift2).astype(BF16)
    xmid_ref[0] = xm
    h2_ref[0] = h2
    q_ref[0] = _dot(h2, wq_ref[...]).astype(BF16)


def _k1_prompt_kernel(x_ref, mod_ref, ng_ref, win_ref, lnag_ref, lnab_ref, ws_ref, bs_ref, cbw_ref, ccw_ref,
                      ccb_ref, lncg_ref, lncb_ref, wbr_ref, wout_ref, wq_ref,
                      xmid_ref, h2_ref, q_ref, hb_ref, hc_ref,
                      ci_buf, glu_buf, yc_buf, mix_buf, *, tt, d):
    t = pl.program_id(1)
    dh = d // 2
    zs = _z_slices(d)
    taps_c = ccw_ref.shape[0]

    @pl.when(t == 0)
    def _():
        ci_buf[0:8, :] = jnp.zeros((8, dh), F32)
        glu_buf[0:32, :] = jnp.zeros((32, dh), F32)

    x = x_ref[0]
    mod = mod_ref[0]
    shift1, scale1 = mod[:, 0:d], mod[:, d:2 * d]
    hb16 = (_rms(x, ng_ref[0:1, :]) * (1.0 + scale1) + shift1).astype(BF16)

    def z(name):
        lo, hi = zs[name]
        return _dot(hb16, win_ref[:, lo:hi])

    ci = z("cb") * z("hb")
    ci_buf[8:8 + tt, :] = ci
    conv_b = (cbw_ref[0:1, :] * ci_buf[6:6 + tt, :] + cbw_ref[1:2, :] * ci_buf[7:7 + tt, :]
              + cbw_ref[2:3, :] * ci)
    yb = z("bb") * conv_b
    merged = _sigmoid(z("g1")) * _dot(yb.astype(BF16), wbr_ref[1])

    glu_buf[32:32 + tt, :] = z("ac") * _sigmoid(z("gc"))
    for r in range(tt // CONV_ROWS):
        base = r * CONV_ROWS + 32 - (taps_c - 1)
        acc = ccb_ref[...] + ccw_ref[0:1, :] * glu_buf[base:base + CONV_ROWS, :]
        for k in range(1, taps_c):
            acc = acc + ccw_ref[k:k + 1, :] * glu_buf[base + k:base + k + CONV_ROWS, :]
        y = _ln(acc, lncg_ref[...], lncb_ref[...])
        yc_buf[r * CONV_ROWS:(r + 1) * CONV_ROWS, :] = y * _sigmoid(y)
    merged = merged + _sigmoid(z("g2")) * _dot(yc_buf[...].astype(BF16), wbr_ref[2])

    chunk = ws_ref.shape[1]
    gdim = dh // ws_ref.shape[0]
    va = _ln(_gelu(z("va")), lnag_ref[...], lnab_ref[...]).astype(BF16)
    for c in range(tt // chunk):
        for g in range(ws_ref.shape[0]):
            mix_buf[c * chunk:(c + 1) * chunk, g * gdim:(g + 1) * gdim] = (
                _dot(ws_ref[g], va[c * chunk:(c + 1) * chunk, g * gdim:(g + 1) * gdim])
                + bs_ref[:, g * gdim:(g + 1) * gdim])
    ya = _gelu(z("ua")) * mix_buf[...]
    merged = merged + _sigmoid(z("g0")) * _dot(ya.astype(BF16), wbr_ref[0])

    _token_tail(x, merged, mod, ng_ref, wout_ref, wq_ref, xmid_ref, h2_ref, q_ref, d)

    @pl.when(t == pl.num_programs(1) - 1)
    def _():
        hb_ref[0] = ci_buf[tt + 6:tt + 8, :]
        hc_ref[0] = glu_buf[tt + 2:tt + 32, :]

    ci_buf[0:8, :] = ci_buf[tt:tt + 8, :]
    glu_buf[0:32, :] = glu_buf[tt:tt + 32, :]


def _k1_sample_kernel(x_ref, mod_ref, ng_ref, win_ref, lnag_ref, lnab_ref, w00_ref, b0_ref, cbw_ref, ccw_ref,
                      ccb_ref, lncg_ref, lncb_ref, wbr_ref, wout_ref, wq_ref, hb0_ref, hb1_ref, hc_hist_ref,
                      xmid_ref, h2_ref, q_ref, va_ref, ci_ref, glu_ref, *, d):
    zs = _z_slices(d)
    taps_c = ccw_ref.shape[0]
    x = x_ref[0]
    mod = mod_ref[0]
    shift1, scale1 = mod[:, 0:d], mod[:, d:2 * d]
    hb16 = (_rms(x, ng_ref[0:1, :]) * (1.0 + scale1) + shift1).astype(BF16)

    def z(name):
        lo, hi = zs[name]
        return _dot(hb16, win_ref[:, lo:hi])

    ci = z("cb") * z("hb")
    ci_ref[...] = ci
    conv_b = cbw_ref[0:1, :] * hb0_ref[...] + cbw_ref[1:2, :] * hb1_ref[...] + cbw_ref[2:3, :] * ci
    yb = z("bb") * conv_b
    merged = _sigmoid(z("g1")) * _dot(yb.astype(BF16), wbr_ref[1])

    glu = z("ac") * _sigmoid(z("gc"))
    glu_ref[...] = glu
    acc = ccb_ref[...] + ccw_ref[taps_c - 1:taps_c, :] * glu
    for k in range(taps_c - 1):
        acc = acc + ccw_ref[k:k + 1, :] * hc_hist_ref[k]
    y = _ln(acc, lncg_ref[...], lncb_ref[...])
    yc = y * _sigmoid(y)
    merged = merged + _sigmoid(z("g2")) * _dot(yc.astype(BF16), wbr_ref[2])

    va = _ln(_gelu(z("va")), lnag_ref[...], lnab_ref[...])
    va_ref[...] = va
    mixed = w00_ref[...].astype(F32) * va.astype(BF16).astype(F32) + b0_ref[...]
    ya = _gelu(z("ua")) * mixed
    merged = merged + _sigmoid(z("g0")) * _dot(ya.astype(BF16), wbr_ref[0])

    _token_tail(x, merged, mod, ng_ref, wout_ref, wq_ref, xmid_ref, h2_ref, q_ref, d)


def _k1_prompt_call(x, mod, lw, *, tt):
    b, t, d = x.shape
    dh = d // 2
    dq = lw["wq"].shape[1]
    kern = functools.partial(_k1_prompt_kernel, tt=tt, d=d)
    consts = [lw["ng"], lw["win"], lw["lnag"], lw["lnab"], lw["ws"], lw["bs"], lw["cbw"], lw["ccw"], lw["ccb"],
              lw["lncg"], lw["lncb"], lw["wbr"], lw["wout"], lw["wq"]]
    return pl.pallas_call(
        kern,
        grid=(b, t // tt),
        in_specs=[pl.BlockSpec((1, tt, d), lambda i, j: (i, j, 0)),
                  pl.BlockSpec((1, 1, mod.shape[-1]), lambda i, j: (i, 0, 0))]
        + [_const_spec(c.shape) for c in consts],
        out_specs=[pl.BlockSpec((1, tt, d), lambda i, j: (i, j, 0)),
                   pl.BlockSpec((1, tt, d), lambda i, j: (i, j, 0)),
                   pl.BlockSpec((1, tt, dq), lambda i, j: (i, j, 0)),
                   pl.BlockSpec((1, 2, dh), lambda i, j: (i, 0, 0)),
                   pl.BlockSpec((1, 30, dh), lambda i, j: (i, 0, 0))],
        out_shape=[jax.ShapeDtypeStruct((b, t, d), F32),
                   jax.ShapeDtypeStruct((b, t, d), BF16),
                   jax.ShapeDtypeStruct((b, t, dq), BF16),
                   jax.ShapeDtypeStruct((b, 2, dh), F32),
                   jax.ShapeDtypeStruct((b, 30, dh), F32)],
        scratch_shapes=[pltpu.VMEM((tt + 8, dh), F32), pltpu.VMEM((tt + 32, dh), F32),
                        pltpu.VMEM((tt, dh), F32), pltpu.VMEM((tt, dh), F32)],
        compiler_params=pltpu.CompilerParams(
            dimension_semantics=("arbitrary", "arbitrary"), vmem_limit_bytes=VMEM_LIMIT_BYTES),
        name="token_prompt",
    )(x, mod.reshape(b, 1, -1), *consts)


def _k1_sample_call(x, mod, lw, hb0, hb1, hc_hist):
    n, d = x.shape
    dh = d // 2
    dq = lw["wq"].shape[1]
    kern = functools.partial(_k1_sample_kernel, d=d)
    consts = [lw["ng"], lw["win"], lw["lnag"], lw["lnab"], lw["w00"], lw["b0"], lw["cbw"], lw["ccw"], lw["ccb"],
              lw["lncg"], lw["lncb"], lw["wbr"], lw["wout"], lw["wq"], hb0, hb1, hc_hist]
    row = lambda w: pl.BlockSpec((n, w), lambda i: (0, 0))
    return pl.pallas_call(
        kern,
        grid=(1,),
        in_specs=[pl.BlockSpec((1, n, d), lambda i: (0, 0, 0)),
                  pl.BlockSpec((1, n, mod.shape[-1]), lambda i: (0, 0, 0))]
        + [_const_spec(c.shape) for c in consts],
        out_specs=[pl.BlockSpec((1, n, d), lambda i: (0, 0, 0)),
                   pl.BlockSpec((1, n, d), lambda i: (0, 0, 0)),
                   pl.BlockSpec((1, n, dq), lambda i: (0, 0, 0)),
                   row(dh), row(dh), row(dh)],
        out_shape=[jax.ShapeDtypeStruct((1, n, d), F32),
                   jax.ShapeDtypeStruct((1, n, d), BF16),
                   jax.ShapeDtypeStruct((1, n, dq), BF16),
                   jax.ShapeDtypeStruct((n, dh), F32),
                   jax.ShapeDtypeStruct((n, dh), F32),
                   jax.ShapeDtypeStruct((n, dh), F32)],
        compiler_params=pltpu.CompilerParams(
            dimension_semantics=("arbitrary",), vmem_limit_bytes=VMEM_LIMIT_BYTES),
        name="token_sample",
    )(x[None], mod[None], *consts)


GELU_C0 = 0.7978845608028654
GELU_C1 = GELU_C0 * 0.044715


def _aligned(x, m):
    return x if isinstance(x, int) else pl.multiple_of(x, m)


def _oddeven_mergesort_pairs(n):
    pairs, p = [], 1
    while p < n:
        k = p
        while k >= 1:
            for j in range(k % p, n - k, 2 * k):
                for i in range(min(k, n - j - k)):
                    if (i + j) // (2 * p) == (i + j + k) // (2 * p):
                        pairs.append((i + j, i + j + k))
            k //= 2
        p *= 2
    return pairs


def _sort_desc(xs):
    xs = list(xs)
    for i, j in _oddeven_mergesort_pairs(len(xs)):
        xs[i], xs[j] = jnp.maximum(xs[i], xs[j]), jnp.minimum(xs[i], xs[j])
    return xs


def _merge_top(xs, ys):
    n = len(xs)
    zs = [jnp.maximum(xs[k], ys[n - 1 - k]) for k in range(n)]
    d = n // 2
    while d >= 1:
        for k in range(n):
            if k & d == 0:
                zs[k], zs[k + d] = jnp.maximum(zs[k], zs[k + d]), jnp.minimum(zs[k], zs[k + d])
        d //= 2
    return zs


def _peer_kernel(h2_ref, q_ref, x_ref, gate_ref, keys_ref, u_ref, vt_ref, fg_ref, out_ref,
                 cnt_s, rk2_s, e1_s, e2_s, raw_s, top_s, rowb_s, a0_s, a1_s, wa_s, acc_s, *, ch, final):
    e = pl.program_id(1)
    n_heads, n_chunks, n_keys, _ = cnt_s.shape
    unit = a0_s.shape[0] * SUBLANES
    rows = unit // n_keys
    rows_per_block = u_ref.shape[0] // n_keys
    upc = u_ref.shape[0] // unit
    n_units = n_chunks * upc
    n_pairs = n_units // 2
    assert upc % 2 == 0 and SUBLANES % rows == 0
    tiles_per_row = n_keys // SUBLANES
    assert tiles_per_row == PEER_TOPK
    btiles_per_row = n_keys // BF16_ROWS
    neg_inf = jnp.float32(-jnp.inf)
    bzero = jnp.zeros((), BF16)

    @pl.when(e == 0)
    def _scores_and_thresholds():
        acc_s[...] = jnp.zeros(acc_s.shape, F32)
        for c in range(n_chunks):
            qc = q_ref[c * ch:(c + 1) * ch, :]
            for h in range(n_heads):
                for p in range(2):
                    col = (2 * h + p) * n_keys
                    s = _nt_dot(keys_ref[h, p], qc[:, col:col + n_keys])
                    raw_s[p, h] = s
                    xs = _sort_desc([s[k * SUBLANES:(k + 1) * SUBLANES, :] for k in range(tiles_per_row)])
                    for shift in (4, 2, 1):
                        xs = _merge_top(xs, [pltpu.roll(v, shift, 0) for v in xs])
                    for r in range(PEER_TOPK):
                        top_s[p, r, h:h + 1, :] = xs[r][0:1, :]
            a = [top_s[0, r] for r in range(PEER_TOPK)]
            b = [top_s[1, r] for r in range(PEER_TOPK)]
            pad = jnp.full(a[0].shape, neg_inf, F32)
            lists = [[a[r] + b[s] for r in range(PEER_TOPK // (s + 1))] for s in range(PEER_TOPK // 2)]
            lists.append([a[0] + b[s] for s in range(PEER_TOPK // 2, PEER_TOPK)])
            lists = [l + [pad] * (PEER_TOPK - len(l)) for l in lists]
            best = functools.reduce(_merge_top, lists)
            top, tau = best[0], best[PEER_TOPK - 1]
            zsum = functools.reduce(lambda acc, v: acc + jnp.exp(v - top), best[1:], jnp.ones_like(top))
            half_zinv = 0.5 / zsum
            kept = [functools.reduce(lambda acc, s: acc + jnp.where(a[r] + b[s] >= tau, 1.0, 0.0),
                                     range(PEER_TOPK // (r + 1)), jnp.zeros_like(tau)) for r in range(PEER_TOPK)]
            for h in range(n_heads):
                hs = slice(h, h + 1)
                a_h = [jnp.broadcast_to(v[hs, :], (BF16_ROWS, ch)) for v in a]
                b_h = [jnp.broadcast_to(v[hs, :], (BF16_ROWS, ch)) for v in b]
                kept_h = [jnp.broadcast_to(v[hs, :], (BF16_ROWS, ch)) for v in kept]
                hz_h = half_zinv[hs, :]
                for t in range(btiles_per_row):
                    ks = slice(t * BF16_ROWS, (t + 1) * BF16_ROWS)
                    s1, s2 = raw_s[0, h, ks, :], raw_s[1, h, ks, :]
                    cnt = jnp.zeros_like(s1)
                    rk2 = jnp.full(s2.shape, float(PEER_TOPK), F32)
                    for r in reversed(range(PEER_TOPK)):
                        cnt = jnp.where(s1 == a_h[r], kept_h[r], cnt)
                        rk2 = jnp.where(s2 == b_h[r], float(r), rk2)
                    cnt_s[h, c, ks, :] = cnt
                    e1_s[h, c, ks, :] = jnp.exp(s1 - a_h[0])
                    rk2_s[h, c, t] = rk2.astype(BF16)
                    e2_s[h, c, t] = (jnp.exp(s2 - b_h[0]) * hz_h).astype(BF16)

    def unit_pos(n):
        if isinstance(n, int):
            return n // upc, n % upc
        c = lax.div(n, jnp.int32(upc))
        return c, n - c * upc

    def stage_a(n, a_s):
        c, k = unit_pos(n)
        prod = _nt_dot(u_ref[pl.ds(_aligned(k * unit, unit), unit), :],
                       h2_ref[pl.ds(_aligned(c * ch, ch), ch), :])
        a_s[...] = prod.reshape(unit // SUBLANES, SUBLANES, ch)

    def stage_b(n, a_s, slot, half):
        c, k = unit_pos(n)
        i0 = e * rows_per_block + k * rows
        i8 = pl.multiple_of(lax.shift_right_logical(i0, 3) * SUBLANES, SUBLANES)
        to_top = (SUBLANES - (i0 & (SUBLANES - 1))) & (SUBLANES - 1)
        for h in range(n_heads):
            for which, ref in enumerate((cnt_s, e1_s)):
                tile = pltpu.roll(ref[h, c, pl.ds(i8, SUBLANES), :], to_top, 0)
                for r in range(rows):
                    rowb_s[half, which, r, h] = jnp.broadcast_to(tile[r:r + 1, :], (BF16_ROWS, ch)).astype(BF16)
        n_rows = PEER_J_TILES * BF16_ROWS
        for lt in range(ch // LANES):
            ls = slice(lt * LANES, (lt + 1) * LANES)
            for jq in range(btiles_per_row // PEER_J_TILES):
                js = slice(jq * PEER_J_TILES, (jq + 1) * PEER_J_TILES)
                wsum = [None] * rows
                for h in range(n_heads):
                    rk2t = rk2_s[h, c, js, :, ls]
                    e2t = e2_s[h, c, js, :, ls]
                    for r in range(rows):
                        keep = rk2t < rowb_s[half, 0, r, h, :, ls]
                        term = rowb_s[half, 1, r, h, :, ls] * jnp.where(keep, e2t, bzero)
                        wsum[r] = term if wsum[r] is None else wsum[r] + term
                for r in range(rows):
                    j0 = r * n_keys + jq * n_rows
                    av = a_s[j0 // SUBLANES:(j0 + n_rows) // SUBLANES, :, ls]
                    y = av * (1.0 + jnp.tanh(av * (GELU_C0 + GELU_C1 * (av * av))))
                    yb = y.reshape(n_rows, LANES).astype(BF16).reshape(PEER_J_TILES, BF16_ROWS, LANES)
                    r0 = half * unit + j0
                    wa_s[slot, r0:r0 + n_rows, ls] = (wsum[r] * yb).reshape(n_rows, LANES)

    def stage_c(pair, slot):
        c, kp = (pair // (upc // 2), pair % (upc // 2)) if isinstance(pair, int) else (
            lax.div(pair, jnp.int32(upc // 2)), lax.rem(pair, jnp.int32(upc // 2)))
        acc_s[c] = acc_s[c] + _dot(vt_ref[kp], wa_s[slot])

    stage_a(0, a0_s)
    wa_s[1] = jnp.zeros(wa_s.shape[1:], BF16)

    def pair_body(m, carry):
        slot = lax.rem(m, 2)
        stage_a(2 * m + 1, a1_s)
        stage_c(jnp.maximum(m - 1, 0), 1 - slot)
        stage_b(2 * m, a0_s, slot, 0)
        stage_a(jnp.minimum(2 * m + 2, n_units - 1), a0_s)
        stage_b(2 * m + 1, a1_s, slot, 1)
        return carry

    lax.fori_loop(0, n_pairs, pair_body, 0)
    stage_c(n_pairs - 1, (n_pairs - 1) % 2)

    @pl.when(e == pl.num_programs(1) - 1)
    def _epilogue():
        for c in range(n_chunks):
            tok = slice(c * ch, (c + 1) * ch)
            y = x_ref[tok, :] + gate_ref[0] * acc_s[c].T
            if final:
                y = _rms(y, fg_ref[...])
            out_ref[tok, :] = y


def _peer_call(h2, q, x, gate, keys, u, vt, fg, *, tile, eb, final):
    n, d = x.shape
    n_heads, _, n_keys, _ = keys.shape
    n_exp = u.shape[0]
    pair = vt.shape[2]
    unit = pair // 2
    ch = min(MXU_COLS, tile)
    n_chunks = tile // ch
    groups, grows, _ = gate.shape
    tiles_per_group = (n // tile) // groups
    kern = functools.partial(_peer_kernel, ch=ch, final=final)
    row_scratch = pltpu.VMEM((n_heads, n_chunks, n_keys, ch), F32)
    tile_scratch = pltpu.VMEM((n_heads, n_chunks, n_keys // BF16_ROWS, BF16_ROWS, ch), BF16)
    prod_scratch = pltpu.VMEM((unit // SUBLANES, SUBLANES, ch), F32)
    return pl.pallas_call(
        kern,
        grid=(n // tile, n_exp // eb),
        in_specs=[pl.BlockSpec((tile, d), lambda t, e: (t, 0)),
                  pl.BlockSpec((tile, q.shape[1]), lambda t, e: (t, 0)),
                  pl.BlockSpec((tile, d), lambda t, e: (t, 0)),
                  pl.BlockSpec((1, grows, d), lambda t, e: (t // tiles_per_group, 0, 0)),
                  _const_spec(keys.shape),
                  pl.BlockSpec((eb, d), lambda t, e: (e, 0)),
                  pl.BlockSpec((eb // pair, d, pair), lambda t, e: (e, 0, 0)),
                  _const_spec(fg.shape)],
        out_specs=pl.BlockSpec((tile, d), lambda t, e: (t, 0)),
        out_shape=jax.ShapeDtypeStruct((n, d), F32),
        scratch_shapes=[row_scratch, tile_scratch, row_scratch, tile_scratch,
                        pltpu.VMEM((2, n_heads, n_keys, ch), F32),
                        pltpu.VMEM((2, PEER_TOPK, n_heads, ch), F32),
                        pltpu.VMEM((2, 2, unit // n_keys, n_heads, BF16_ROWS, ch), BF16),
                        prod_scratch, prod_scratch,
                        pltpu.VMEM((2, pair, ch), BF16),
                        pltpu.VMEM((n_chunks, d, ch), F32)],
        compiler_params=pltpu.CompilerParams(
            dimension_semantics=("arbitrary", "arbitrary"), vmem_limit_bytes=VMEM_LIMIT_BYTES),
        name="peer_final" if final else "peer",
    )(h2, q, x, gate, keys, u, vt, fg)


TOKEN_TILE = 256
PEER_TILE = 512
PEER_EXPERT_BLOCK = 2048
PEER_UNIT_ROWS = 4
PEER_J_TILES = 2


def kernel(x_prompt, x_sample, c_prompt, c_sample, state_conv_b, state_conv_c, norm_g, final_g, ada_w, ada_b, w_in, ln_a_g, ln_a_b, w_spatial, b_spatial, conv_b_w, conv_c_w, conv_c_b, ln_c_g, ln_c_b, w_branch, w_out, peer_w_query, peer_sub_keys, peer_u, peer_v):
    depth = w_in.shape[0]
    bp, seq, d = x_prompt.shape
    ns = x_sample.shape[0]
    dh = d // 2
    groups, chunk = w_spatial.shape[1], w_spatial.shape[2]
    gdim = dh // groups

    mod_all = _mod_call(jnp.concatenate([c_prompt, c_sample], axis=0), ada_w, ada_b)

    tril = jnp.tril(jnp.ones((chunk, chunk), dtype=bool))
    ws_all = jnp.where(tril[None, None], w_spatial, 0.0).astype(BF16)
    bs_all = jnp.repeat(jnp.swapaxes(b_spatial, 1, 2), gdim, axis=2)
    w00_all = jnp.repeat(w_spatial[:, :, 0, 0], gdim, axis=1).astype(BF16)[:, None, :]
    b0_all = jnp.repeat(b_spatial[:, :, 0], gdim, axis=1)[:, None, :]
    win_all = w_in.astype(BF16)
    wbr_all = w_branch.astype(BF16)
    wout_all = w_out.astype(BF16)
    wq_all = peer_w_query.astype(BF16)
    keys_all = peer_sub_keys.astype(BF16)
    u_all = peer_u.astype(BF16)
    n_exp = peer_u.shape[1]
    pair = 2 * PEER_UNIT_ROWS * peer_sub_keys.shape[3]
    vt_all = jnp.swapaxes(peer_v.reshape(depth, n_exp // pair, pair, d), 2, 3).astype(BF16)
    hc_hist_all = jnp.swapaxes(state_conv_c, 1, 2)
    fg = final_g[None, :]

    xp = x_prompt
    xs = x_sample.reshape(ns, d)
    v_s, hb_p, hb_s, hc_p, hc_s = [], [], [], [], []
    for l in range(depth):
        lw = dict(ng=norm_g[l], win=win_all[l], lnag=ln_a_g[l][None], lnab=ln_a_b[l][None], ws=ws_all[l],
                  bs=bs_all[l], w00=w00_all[l], b0=b0_all[l], cbw=conv_b_w[l], ccw=conv_c_w[l],
                  ccb=conv_c_b[l][None], lncg=ln_c_g[l][None], lncb=ln_c_b[l][None], wbr=wbr_all[l],
                  wout=wout_all[l], wq=wq_all[l])
        mod_p, mod_s = mod_all[l, :bp], mod_all[l, bp:]
        final = l == depth - 1

        xmid, h2, q, hb, hc = _k1_prompt_call(xp, mod_p, lw, tt=TOKEN_TILE)
        hb_p.append(hb)
        hc_p.append(hc)
        xp = _peer_call(h2.reshape(bp * seq, d), q.reshape(bp * seq, -1), xmid.reshape(bp * seq, d),
                        mod_p[:, None, 5 * d:], keys_all[l], u_all[l], vt_all[l], fg,
                        tile=PEER_TILE, eb=PEER_EXPERT_BLOCK, final=final).reshape(bp, seq, d)

        xmid, h2, q, va, ci, glu = _k1_sample_call(xs, mod_s, lw, state_conv_b[l, :, 0], state_conv_b[l, :, 1],
                                                   hc_hist_all[l])
        v_s.append(va[:, None, :])
        hb_s.append(jnp.stack([state_conv_b[l, :, 1], ci], axis=1))
        hc_s.append(jnp.concatenate([state_conv_c[l, :, 1:], glu[:, None, :]], axis=1))
        xs = _peer_call(h2[0], q[0], xmid[0], mod_s[None, :, 5 * d:], keys_all[l], u_all[l], vt_all[l], fg,
                        tile=ns, eb=PEER_EXPERT_BLOCK, final=final)

    return (xp, xs.reshape(ns, 1, d), jnp.stack(v_s), jnp.stack(hb_p), jnp.stack(hb_s), jnp.stack(hc_p),
            jnp.stack(hc_s))
```

```python
import functools

import jax
import jax.numpy as jnp
from jax import lax
from jax.experimental import pallas as pl
from jax.experimental.pallas import tpu as pltpu

F32 = jnp.float32
BF16 = jnp.bfloat16
EPS = 1e-6

VMEM_LIMIT_BYTES = 56 * 1024 * 1024
LANES = 128
SUBLANES = 8
MXU_COLS = 256

PEER_TOPK = 16
CONV_ROWS = 32


def _nt_dot(a, b):
    return lax.dot_general(a, b, (((1,), (1,)), ((), ())), preferred_element_type=F32)


def _dot(a, b):
    return jnp.dot(a, b, preferred_element_type=F32)


def _rms(x, g):
    return x * lax.rsqrt(jnp.mean(x * x, axis=-1, keepdims=True) + EPS) * g


def _ln(x, g, b):
    mu = jnp.mean(x, axis=-1, keepdims=True)
    xc = x - mu
    var = jnp.mean(xc * xc, axis=-1, keepdims=True)
    return xc * lax.rsqrt(var + EPS) * g + b


def _gelu(x):
    return 0.5 * x * (1.0 + jnp.tanh(0.7978845608028654 * (x + 0.044715 * (x * x * x))))


def _sigmoid(x):
    return 1.0 / (1.0 + jnp.exp(-x))


def _const_spec(shape):
    nd = len(shape)
    return pl.BlockSpec(shape, lambda *_: (0,) * nd, pipeline_mode=pl.Buffered(1))


def _mod_kernel(c_ref, w_ref, b_ref, o_ref):
    c = c_ref[...]
    s = (c * _sigmoid(c)).astype(BF16)
    o_ref[0] = _dot(s, w_ref[0].astype(BF16)) + b_ref[0]


def _mod_call(c_all, ada_w, ada_b):
    depth, d, n = ada_w.shape
    m = c_all.shape[0]
    tn = 1536
    return pl.pallas_call(
        _mod_kernel,
        grid=(depth, n // tn),
        in_specs=[
            pl.BlockSpec((m, d), lambda l, j: (0, 0)),
            pl.BlockSpec((1, d, tn), lambda l, j: (l, 0, j)),
            pl.BlockSpec((1, 1, tn), lambda l, j: (l, 0, j)),
        ],
        out_specs=pl.BlockSpec((1, m, tn), lambda l, j: (l, 0, j)),
        out_shape=jax.ShapeDtypeStruct((depth, m, n), F32),
        compiler_params=pltpu.CompilerParams(
            dimension_semantics=("arbitrary", "arbitrary"), vmem_limit_bytes=VMEM_LIMIT_BYTES),
        name="adaln_mod",
    )(c_all, ada_w, ada_b.reshape(depth, 1, n))


def _z_slices(d):
    h = d // 2
    names = ["ua", "va", "hb", "bb", "cb", "ac", "gc"]
    out, lo = {}, 0
    for n in names:
        out[n] = (lo, lo + h)
        lo += h
    for k in range(3):
        out[f"g{k}"] = (lo, lo + d)
        lo += d
    return out


def _token_tail(x, merged, mod, ng_ref, wout_ref, wq_ref, xmid_ref, h2_ref, q_ref, d):
    gate1 = mod[:, 2 * d:3 * d]
    shift2, scale2 = mod[:, 3 * d:4 * d], mod[:, 4 * d:5 * d]
    xm = x + gate1 * _dot(merged.astype(BF16), wout_ref[...])
    h2 = (_rms(xm, ng_ref[1:2, :]) * (1.0 + scale2) + shift2).astype(BF16)
    xmid_ref[0] = xm
    h2_ref[0] = h2
    q_ref[0] = _dot(h2, wq_ref[...]).astype(BF16)


def _k1_prompt_kernel(x_ref, mod_ref, ng_ref, win_ref, lnag_ref, lnab_ref, ws_ref, bs_ref, cbw_ref, ccw_ref,
                      ccb_ref, lncg_ref, lncb_ref, wbr_ref, wout_ref, wq_ref,
                      xmid_ref, h2_ref, q_ref, hb_ref, hc_ref,
                      ci_buf, glu_buf, yc_buf, mix_buf, *, tt, d):
    t = pl.program_id(1)
    dh = d // 2
    zs = _z_slices(d)
    taps_c = ccw_ref.shape[0]

    @pl.when(t == 0)
    def _():
        ci_buf[0:8, :] = jnp.zeros((8, dh), F32)
        glu_buf[0:32, :] = jnp.zeros((32, dh), F32)

    x = x_ref[0]
    mod = mod_ref[0]
    shift1, scale1 = mod[:, 0:d], mod[:, d:2 * d]
    hb16 = (_rms(x, ng_ref[0:1, :]) * (1.0 + scale1) + shift1).astype(BF16)

    def z(name):
        lo, hi = zs[name]
        return _dot(hb16, win_ref[:, lo:hi])

    ci = z("cb") * z("hb")
    ci_buf[8:8 + tt, :] = ci
    conv_b = (cbw_ref[0:1, :] * ci_buf[6:6 + tt, :] + cbw_ref[1:2, :] * ci_buf[7:7 + tt, :]
              + cbw_ref[2:3, :] * ci)
    yb = z("bb") * conv_b
    merged = _sigmoid(z("g1")) * _dot(yb.astype(BF16), wbr_ref[1])

    glu_buf[32:32 + tt, :] = z("ac") * _sigmoid(z("gc"))
    for r in range(tt // CONV_ROWS):
        base = r * CONV_ROWS + 32 - (taps_c - 1)
        acc = ccb_ref[...] + ccw_ref[0:1, :] * glu_buf[base:base + CONV_ROWS, :]
        for k in range(1, taps_c):
            acc = acc + ccw_ref[k:k + 1, :] * glu_buf[base + k:base + k + CONV_ROWS, :]
        y = _ln(acc, lncg_ref[...], lncb_ref[...])
        yc_buf[r * CONV_ROWS:(r + 1) * CONV_ROWS, :] = y * _sigmoid(y)
    merged = merged + _sigmoid(z("g2")) * _dot(yc_buf[...].astype(BF16), wbr_ref[2])

    chunk = ws_ref.shape[1]
    gdim = dh // ws_ref.shape[0]
    va = _ln(_gelu(z("va")), lnag_ref[...], lnab_ref[...]).astype(BF16)
    for c in range(tt // chunk):
        for g in range(ws_ref.shape[0]):
            mix_buf[c * chunk:(c + 1) * chunk, g * gdim:(g + 1) * gdim] = (
                _dot(ws_ref[g], va[c * chunk:(c + 1) * chunk, g * gdim:(g + 1) * gdim])
                + bs_ref[:, g * gdim:(g + 1) * gdim])
    ya = _gelu(z("ua")) * mix_buf[...]
    merged = merged + _sigmoid(z("g0")) * _dot(ya.astype(BF16), wbr_ref[0])

    _token_tail(x, merged, mod, ng_ref, wout_ref, wq_ref, xmid_ref, h2_ref, q_ref, d)

    @pl.when(t == pl.num_programs(1) - 1)
    def _():
        hb_ref[0] = ci_buf[tt + 6:tt + 8, :]
        hc_ref[0] = glu_buf[tt + 2:tt + 32, :]

    ci_buf[0:8, :] = ci_buf[tt:tt + 8, :]
    glu_buf[0:32, :] = glu_buf[tt:tt + 32, :]


def _k1_sample_kernel(x_ref, mod_ref, ng_ref, win_ref, lnag_ref, lnab_ref, w00_ref, b0_ref, cbw_ref, ccw_ref,
                      ccb_ref, lncg_ref, lncb_ref, wbr_ref, wout_ref, wq_ref, hb0_ref, hb1_ref, hc_hist_ref,
                      xmid_ref, h2_ref, q_ref, va_ref, ci_ref, glu_ref, *, d):
    zs = _z_slices(d)
    taps_c = ccw_ref.shape[0]
    x = x_ref[0]
    mod = mod_ref[0]
    shift1, scale1 = mod[:, 0:d], mod[:, d:2 * d]
    hb16 = (_rms(x, ng_ref[0:1, :]) * (1.0 + scale1) + shift1).astype(BF16)

    def z(name):
        lo, hi = zs[name]
        return _dot(hb16, win_ref[:, lo:hi])

    ci = z("cb") * z("hb")
    ci_ref[...] = ci
    conv_b = cbw_ref[0:1, :] * hb0_ref[...] + cbw_ref[1:2, :] * hb1_ref[...] + cbw_ref[2:3, :] * ci
    yb = z("bb") * conv_b
    merged = _sigmoid(z("g1")) * _dot(yb.astype(BF16), wbr_ref[1])

    glu = z("ac") * _sigmoid(z("gc"))
    glu_ref[...] = glu
    acc = ccb_ref[...] + ccw_ref[taps_c - 1:taps_c, :] * glu
    for k in range(taps_c - 1):
        acc = acc + ccw_ref[k:k + 1, :] * hc_hist_ref[k]
    y = _ln(acc, lncg_ref[...], lncb_ref[...])
    yc = y * _sigmoid(y)
    merged = merged + _sigmoid(z("g2")) * _dot(yc.astype(BF16), wbr_ref[2])

    va = _ln(_gelu(z("va")), lnag_ref[...], lnab_ref[...])
    va_ref[...] = va
    mixed = w00_ref[...].astype(F32) * va.astype(BF16).astype(F32) + b0_ref[...]
    ya = _gelu(z("ua")) * mixed
    merged = merged + _sigmoid(z("g0")) * _dot(ya.astype(BF16), wbr_ref[0])

    _token_tail(x, merged, mod, ng_ref, wout_ref, wq_ref, xmid_ref, h2_ref, q_ref, d)


def _k1_prompt_call(x, mod, lw, *, tt):
    b, t, d = x.shape
    dh = d // 2
    dq = lw["wq"].shape[1]
    kern = functools.partial(_k1_prompt_kernel, tt=tt, d=d)
    consts = [lw["ng"], lw["win"], lw["lnag"], lw["lnab"], lw["ws"], lw["bs"], lw["cbw"], lw["ccw"], lw["ccb"],
              lw["lncg"], lw["lncb"], lw["wbr"], lw["wout"], lw["wq"]]
    return pl.pallas_call(
        kern,
        grid=(b, t // tt),
        in_specs=[pl.BlockSpec((1, tt, d), lambda i, j: (i, j, 0)),
                  pl.BlockSpec((1, 1, mod.shape[-1]), lambda i, j: (i, 0, 0))]
        + [_const_spec(c.shape) for c in consts],
        out_specs=[pl.BlockSpec((1, tt, d), lambda i, j: (i, j, 0)),
                   pl.BlockSpec((1, tt, d), lambda i, j: (i, j, 0)),
                   pl.BlockSpec((1, tt, dq), lambda i, j: (i, j, 0)),
                   pl.BlockSpec((1, 2, dh), lambda i, j: (i, 0, 0)),
                   pl.BlockSpec((1, 30, dh), lambda i, j: (i, 0, 0))],
        out_shape=[jax.ShapeDtypeStruct((b, t, d), F32),
                   jax.ShapeDtypeStruct((b, t, d), BF16),
                   jax.ShapeDtypeStruct((b, t, dq), BF16),
                   jax.ShapeDtypeStruct((b, 2, dh), F32),
                   jax.ShapeDtypeStruct((b, 30, dh), F32)],
        scratch_shapes=[pltpu.VMEM((tt + 8, dh), F32), pltpu.VMEM((tt + 32, dh), F32),
                        pltpu.VMEM((tt, dh), F32), pltpu.VMEM((tt, dh), F32)],
        compiler_params=pltpu.CompilerParams(
            dimension_semantics=("arbitrary", "arbitrary"), vmem_limit_bytes=VMEM_LIMIT_BYTES),
        name="token_prompt",
    )(x, mod.reshape(b, 1, -1), *consts)


def _k1_sample_call(x, mod, lw, hb0, hb1, hc_hist):
    n, d = x.shape
    dh = d // 2
    dq = lw["wq"].shape[1]
    kern = functools.partial(_k1_sample_kernel, d=d)
    consts = [lw["ng"], lw["win"], lw["lnag"], lw["lnab"], lw["w00"], lw["b0"], lw["cbw"], lw["ccw"], lw["ccb"],
              lw["lncg"], lw["lncb"], lw["wbr"], lw["wout"], lw["wq"], hb0, hb1, hc_hist]
    row = lambda w: pl.BlockSpec((n, w), lambda i: (0, 0))
    return pl.pallas_call(
        kern,
        grid=(1,),
        in_specs=[pl.BlockSpec((1, n, d), lambda i: (0, 0, 0)),
                  pl.BlockSpec((1, n, mod.shape[-1]), lambda i: (0, 0, 0))]
        + [_const_spec(c.shape) for c in consts],
        out_specs=[pl.BlockSpec((1, n, d), lambda i: (0, 0, 0)),
                   pl.BlockSpec((1, n, d), lambda i: (0, 0, 0)),
                   pl.BlockSpec((1, n, dq), lambda i: (0, 0, 0)),
                   row(dh), row(dh), row(dh)],
        out_shape=[jax.ShapeDtypeStruct((1, n, d), F32),
                   jax.ShapeDtypeStruct((1, n, d), BF16),
                   jax.ShapeDtypeStruct((1, n, dq), BF16),
                   jax.ShapeDtypeStruct((n, dh), F32),
                   jax.ShapeDtypeStruct((n, dh), F32),
                   jax.ShapeDtypeStruct((n, dh), F32)],
        compiler_params=pltpu.CompilerParams(
            dimension_semantics=("arbitrary",), vmem_limit_bytes=VMEM_LIMIT_BYTES),
        name="token_sample",
    )(x[None], mod[None], *consts)


GELU_C0 = 0.7978845608028654
GELU_C1 = GELU_C0 * 0.044715
TAU_MARGIN = 2.0 ** -21


def _aligned(x, m):
    return x if isinstance(x, int) else pl.multiple_of(x, m)


def _oddeven_mergesort_pairs(n):
    pairs, p = [], 1
    while p < n:
        k = p
        while k >= 1:
            for j in range(k % p, n - k, 2 * k):
                for i in range(min(k, n - j - k)):
                    if (i + j) // (2 * p) == (i + j + k) // (2 * p):
                        pairs.append((i + j, i + j + k))
            k //= 2
        p *= 2
    return pairs


def _sort_desc(xs):
    xs = list(xs)
    for i, j in _oddeven_mergesort_pairs(len(xs)):
        xs[i], xs[j] = jnp.maximum(xs[i], xs[j]), jnp.minimum(xs[i], xs[j])
    return xs


def _merge_top(xs, ys):
    n = len(xs)
    zs = [jnp.maximum(xs[k], ys[n - 1 - k]) for k in range(n)]
    d = n // 2
    while d >= 1:
        for k in range(n):
            if k & d == 0:
                zs[k], zs[k + d] = jnp.maximum(zs[k], zs[k + d]), jnp.minimum(zs[k], zs[k + d])
        d //= 2
    return zs


def _peer_kernel(h2_ref, q_ref, x_ref, gate_ref, keys_ref, u_ref, vt_ref, fg_ref, out_ref,
                 th_s, s2_s, e1_s, e2_s, top_s, rowb_s, a0_s, a1_s, wa_s, acc_s, *, ch, final):
    e = pl.program_id(1)
    n_heads, n_chunks, n_keys, _ = th_s.shape
    unit = a0_s.shape[0] * SUBLANES
    rows = unit // n_keys
    rows_per_block = u_ref.shape[0] // n_keys
    upc = u_ref.shape[0] // unit
    n_units = n_chunks * upc
    n_pairs = n_units // 2
    assert upc % 2 == 0 and SUBLANES % rows == 0
    tiles_per_row = n_keys // SUBLANES
    assert tiles_per_row == PEER_TOPK
    neg_inf = jnp.float32(-jnp.inf)

    @pl.when(e == 0)
    def _scores_and_thresholds():
        acc_s[...] = jnp.zeros(acc_s.shape, F32)
        for c in range(n_chunks):
            qc = q_ref[c * ch:(c + 1) * ch, :]
            for h in range(n_heads):
                for p in range(2):
                    col = (2 * h + p) * n_keys
                    s = _nt_dot(keys_ref[h, p], qc[:, col:col + n_keys])
                    if p == 0:
                        th_s[h, c] = s
                    else:
                        s2_s[h, c] = s.reshape(tiles_per_row, SUBLANES, ch)
                    xs = _sort_desc([s[k * SUBLANES:(k + 1) * SUBLANES, :] for k in range(tiles_per_row)])
                    for shift in (4, 2, 1):
                        xs = _merge_top(xs, [pltpu.roll(v, shift, 0) for v in xs])
                    for r in range(PEER_TOPK):
                        top_s[p, r, h:h + 1, :] = xs[r][0:1, :]
            a = [top_s[0, r] for r in range(PEER_TOPK)]
            b = [top_s[1, r] for r in range(PEER_TOPK)]
            pad = jnp.full(a[0].shape, neg_inf, F32)
            lists = [[a[r] + b[s] for r in range(PEER_TOPK // (s + 1))] for s in range(PEER_TOPK // 2)]
            lists.append([a[0] + b[s] for s in range(PEER_TOPK // 2, PEER_TOPK)])
            lists = [l + [pad] * (PEER_TOPK - len(l)) for l in lists]
            best = functools.reduce(_merge_top, lists)
            top = best[0]
            tau = best[PEER_TOPK - 1] - jnp.abs(best[PEER_TOPK - 1]) * TAU_MARGIN
            zsum = functools.reduce(lambda acc, v: acc + jnp.exp(v - top), best[1:], jnp.ones_like(top))
            half_zinv = 0.5 / zsum
            for h in range(n_heads):
                hs = slice(h, h + 1)
                s1 = th_s[h, c]
                e1_s[h, c] = jnp.exp(s1 - a[0][hs, :])
                th_s[h, c] = tau[hs, :] - s1
                e2_s[h, c] = jnp.exp(s2_s[h, c] - b[0][hs, :]) * half_zinv[hs, :]

    def unit_pos(n):
        if isinstance(n, int):
            return n // upc, n % upc
        c = lax.div(n, jnp.int32(upc))
        return c, n - c * upc

    def stage_a(n, a_s):
        c, k = unit_pos(n)
        prod = _nt_dot(u_ref[pl.ds(_aligned(k * unit, unit), unit), :],
                       h2_ref[pl.ds(_aligned(c * ch, ch), ch), :])
        a_s[...] = prod.reshape(unit // SUBLANES, SUBLANES, ch)

    def stage_b(n, a_s, slot, half):
        c, k = unit_pos(n)
        i0 = e * rows_per_block + k * rows
        i8 = pl.multiple_of(lax.shift_right_logical(i0, 3) * SUBLANES, SUBLANES)
        to_top = (SUBLANES - (i0 & (SUBLANES - 1))) & (SUBLANES - 1)
        for h in range(n_heads):
            for which, ref in enumerate((th_s, e1_s)):
                tile = pltpu.roll(ref[h, c, pl.ds(i8, SUBLANES), :], to_top, 0)
                for r in range(rows):
                    rowb_s[half, which, r, h] = jnp.broadcast_to(tile[r:r + 1, :], (SUBLANES, ch))
        for lt in range(ch // LANES):
            ls = slice(lt * LANES, (lt + 1) * LANES)
            for jq in range(tiles_per_row // PEER_J_TILES):
                js = slice(jq * PEER_J_TILES, (jq + 1) * PEER_J_TILES)
                wsum = [None] * rows
                for h in range(n_heads):
                    s2t = s2_s[h, c, js, :, ls]
                    e2t = e2_s[h, c, js, :, ls]
                    for r in range(rows):
                        keep = s2t >= rowb_s[half, 0, r, h, :, ls]
                        term = rowb_s[half, 1, r, h, :, ls] * jnp.where(keep, e2t, 0.0)
                        wsum[r] = term if wsum[r] is None else wsum[r] + term
                for r in range(rows):
                    t0 = r * tiles_per_row + jq * PEER_J_TILES
                    av = a_s[t0:t0 + PEER_J_TILES, :, ls]
                    y = av * (1.0 + jnp.tanh(av * (GELU_C0 + GELU_C1 * (av * av))))
                    r0 = half * unit + t0 * SUBLANES
                    wa_s[slot, r0:r0 + PEER_J_TILES * SUBLANES, ls] = (
                        (wsum[r] * y).reshape(PEER_J_TILES * SUBLANES, LANES).astype(BF16))

    def stage_c(pair, slot):
        c, kp = (pair // (upc // 2), pair % (upc // 2)) if isinstance(pair, int) else (
            lax.div(pair, jnp.int32(upc // 2)), lax.rem(pair, jnp.int32(upc // 2)))
        acc_s[c] = acc_s[c] + _dot(vt_ref[kp], wa_s[slot])

    stage_a(0, a0_s)
    wa_s[1] = jnp.zeros(wa_s.shape[1:], BF16)

    def pair_body(m, carry):
        slot = lax.rem(m, 2)
        stage_a(2 * m + 1, a1_s)
        stage_c(jnp.maximum(m - 1, 0), 1 - slot)
        stage_b(2 * m, a0_s, slot, 0)
        stage_a(jnp.minimum(2 * m + 2, n_units - 1), a0_s)
        stage_b(2 * m + 1, a1_s, slot, 1)
        return carry

    lax.fori_loop(0, n_pairs, pair_body, 0)
    stage_c(n_pairs - 1, (n_pairs - 1) % 2)

    @pl.when(e == pl.num_programs(1) - 1)
    def _epilogue():
        for c in range(n_chunks):
            tok = slice(c * ch, (c + 1) * ch)
            y = x_ref[tok, :] + gate_ref[0] * acc_s[c].T
            if final:
                y = _rms(y, fg_ref[...])
            out_ref[tok, :] = y


def _peer_call(h2, q, x, gate, keys, u, vt, fg, *, tile, eb, final):
    n, d = x.shape
    n_heads, _, n_keys, _ = keys.shape
    n_exp = u.shape[0]
    pair = vt.shape[2]
    unit = pair // 2
    ch = min(MXU_COLS, tile)
    n_chunks = tile // ch
    groups, grows, _ = gate.shape
    tiles_per_group = (n // tile) // groups
    kern = functools.partial(_peer_kernel, ch=ch, final=final)
    row_scratch = pltpu.VMEM((n_heads, n_chunks, n_keys, ch), F32)
    tile_scratch = pltpu.VMEM((n_heads, n_chunks, n_keys // SUBLANES, SUBLANES, ch), F32)
    prod_scratch = pltpu.VMEM((unit // SUBLANES, SUBLANES, ch), F32)
    return pl.pallas_call(
        kern,
        grid=(n // tile, n_exp // eb),
        in_specs=[pl.BlockSpec((tile, d), lambda t, e: (t, 0)),
                  pl.BlockSpec((tile, q.shape[1]), lambda t, e: (t, 0)),
                  pl.BlockSpec((tile, d), lambda t, e: (t, 0)),
                  pl.BlockSpec((1, grows, d), lambda t, e: (t // tiles_per_group, 0, 0)),
                  _const_spec(keys.shape),
                  pl.BlockSpec((eb, d), lambda t, e: (e, 0)),
                  pl.BlockSpec((eb // pair, d, pair), lambda t, e: (e, 0, 0)),
                  _const_spec(fg.shape)],
        out_specs=pl.BlockSpec((tile, d), lambda t, e: (t, 0)),
        out_shape=jax.ShapeDtypeStruct((n, d), F32),
        scratch_shapes=[row_scratch, tile_scratch, row_scratch, tile_scratch,
                        pltpu.VMEM((2, PEER_TOPK, n_heads, ch), F32),
                        pltpu.VMEM((2, 2, unit // n_keys, n_heads, SUBLANES, ch), F32),
                        prod_scratch, prod_scratch,
                        pltpu.VMEM((2, pair, ch), BF16),
                        pltpu.VMEM((n_chunks, d, ch), F32)],
        compiler_params=pltpu.CompilerParams(
            dimension_semantics=("arbitrary", "arbitrary"), vmem_limit_bytes=VMEM_LIMIT_BYTES),
        name="peer_final" if final else "peer",
    )(h2, q, x, gate, keys, u, vt, fg)


TOKEN_TILE = 256
PEER_TILE = 512
PEER_EXPERT_BLOCK = 2048
PEER_UNIT_ROWS = 4
PEER_J_TILES = 4


def kernel(x_prompt, x_sample, c_prompt, c_sample, state_conv_b, state_conv_c, norm_g, final_g, ada_w, ada_b, w_in, ln_a_g, ln_a_b, w_spatial, b_spatial, conv_b_w, conv_c_w, conv_c_b, ln_c_g, ln_c_b, w_branch, w_out, peer_w_query, peer_sub_keys, peer_u, peer_v):
    depth = w_in.shape[0]
    bp, seq, d = x_prompt.shape
    ns = x_sample.shape[0]
    dh = d // 2
    groups, chunk = w_spatial.shape[1], w_spatial.shape[2]
    gdim = dh // groups

    mod_all = _mod_call(jnp.concatenate([c_prompt, c_sample], axis=0), ada_w, ada_b)

    tril = jnp.tril(jnp.ones((chunk, chunk), dtype=bool))
    ws_all = jnp.where(tril[None, None], w_spatial, 0.0).astype(BF16)
    bs_all = jnp.repeat(jnp.swapaxes(b_spatial, 1, 2), gdim, axis=2)
    w00_all = jnp.repeat(w_spatial[:, :, 0, 0], gdim, axis=1).astype(BF16)[:, None, :]
    b0_all = jnp.repeat(b_spatial[:, :, 0], gdim, axis=1)[:, None, :]
    win_all = w_in.astype(BF16)
    wbr_all = w_branch.astype(BF16)
    wout_all = w_out.astype(BF16)
    wq_all = peer_w_query.astype(BF16)
    keys_all = peer_sub_keys.astype(BF16)
    u_all = peer_u.astype(BF16)
    n_exp = peer_u.shape[1]
    pair = 2 * PEER_UNIT_ROWS * peer_sub_keys.shape[3]
    vt_all = jnp.swapaxes(peer_v.reshape(depth, n_exp // pair, pair, d), 2, 3).astype(BF16)
    hc_hist_all = jnp.swapaxes(state_conv_c, 1, 2)
    fg = final_g[None, :]

    xp = x_prompt
    xs = x_sample.reshape(ns, d)
    v_s, hb_p, hb_s, hc_p, hc_s = [], [], [], [], []
    for l in range(depth):
        lw = dict(ng=norm_g[l], win=win_all[l], lnag=ln_a_g[l][None], lnab=ln_a_b[l][None], ws=ws_all[l],
                  bs=bs_all[l], w00=w00_all[l], b0=b0_all[l], cbw=conv_b_w[l], ccw=conv_c_w[l],
                  ccb=conv_c_b[l][None], lncg=ln_c_g[l][None], lncb=ln_c_b[l][None], wbr=wbr_all[l],
                  wout=wout_all[l], wq=wq_all[l])
        mod_p, mod_s = mod_all[l, :bp], mod_all[l, bp:]
        final = l == depth - 1

        xmid, h2, q, hb, hc = _k1_prompt_call(xp, mod_p, lw, tt=TOKEN_TILE)
        hb_p.append(hb)
        hc_p.append(hc)
        xp = _peer_call(h2.reshape(bp * seq, d), q.reshape(bp * seq, -1), xmid.reshape(bp * seq, d),
                        mod_p[:, None, 5 * d:], keys_all[l], u_all[l], vt_all[l], fg,
                        tile=PEER_TILE, eb=PEER_EXPERT_BLOCK, final=final).reshape(bp, seq, d)

        xmid, h2, q, va, ci, glu = _k1_sample_call(xs, mod_s, lw, state_conv_b[l, :, 0], state_conv_b[l, :, 1],
                                                   hc_hist_all[l])
        v_s.append(va[:, None, :])
        hb_s.append(jnp.stack([state_conv_b[l, :, 1], ci], axis=1))
        hc_s.append(jnp.concatenate([state_conv_c[l, :, 1:], glu[:, None, :]], axis=1))
        xs = _peer_call(h2[0], q[0], xmid[0], mod_s[None, :, 5 * d:], keys_all[l], u_all[l], vt_all[l], fg,
                        tile=ns, eb=PEER_EXPERT_BLOCK, final=final)

    return (xp, xs.reshape(ns, 1, d), jnp.stack(v_s), jnp.stack(hb_p), jnp.stack(hb_s), jnp.stack(hc_p),
            jnp.stack(hc_s))
```

```python
import functools

import jax
import jax.numpy as jnp
from jax import lax
from jax.experimental import pallas as pl
from jax.experimental.pallas import tpu as pltpu

F32 = jnp.float32
BF16 = jnp.bfloat16
EPS = 1e-6

VMEM_LIMIT_BYTES = 56 * 1024 * 1024
LANES = 128
SUBLANES = 8
MXU_COLS = 256

PEER_TOPK = 16
CONV_ROWS = 32


def _nt_dot(a, b):
    return lax.dot_general(a, b, (((1,), (1,)), ((), ())), preferred_element_type=F32)


def _dot(a, b):
    return jnp.dot(a, b, preferred_element_type=F32)


def _rms(x, g):
    return x * lax.rsqrt(jnp.mean(x * x, axis=-1, keepdims=True) + EPS) * g


def _ln(x, g, b):
    mu = jnp.mean(x, axis=-1, keepdims=True)
    xc = x - mu
    var = jnp.mean(xc * xc, axis=-1, keepdims=True)
    return xc * lax.rsqrt(var + EPS) * g + b


def _gelu(x):
    return 0.5 * x * (1.0 + jnp.tanh(0.7978845608028654 * (x + 0.044715 * (x * x * x))))


def _sigmoid(x):
    return 1.0 / (1.0 + jnp.exp(-x))


def _const_spec(shape):
    nd = len(shape)
    return pl.BlockSpec(shape, lambda *_: (0,) * nd, pipeline_mode=pl.Buffered(1))


def _mod_kernel(c_ref, w_ref, b_ref, o_ref):
    c = c_ref[...]
    s = (c * _sigmoid(c)).astype(BF16)
    o_ref[0] = _dot(s, w_ref[0].astype(BF16)) + b_ref[0]


def _mod_call(c_all, ada_w, ada_b):
    depth, d, n = ada_w.shape
    m = c_all.shape[0]
    tn = 1536
    return pl.pallas_call(
        _mod_kernel,
        grid=(depth, n // tn),
        in_specs=[
            pl.BlockSpec((m, d), lambda l, j: (0, 0)),
            pl.BlockSpec((1, d, tn), lambda l, j: (l, 0, j)),
            pl.BlockSpec((1, 1, tn), lambda l, j: (l, 0, j)),
        ],
        out_specs=pl.BlockSpec((1, m, tn), lambda l, j: (l, 0, j)),
        out_shape=jax.ShapeDtypeStruct((depth, m, n), F32),
        compiler_params=pltpu.CompilerParams(
            dimension_semantics=("arbitrary", "arbitrary"), vmem_limit_bytes=VMEM_LIMIT_BYTES),
        name="adaln_mod",
    )(c_all, ada_w, ada_b.reshape(depth, 1, n))


def _z_slices(d):
    h = d // 2
    names = ["ua", "va", "hb", "bb", "cb", "ac", "gc"]
    out, lo = {}, 0
    for n in names:
        out[n] = (lo, lo + h)
        lo += h
    for k in range(3):
        out[f"g{k}"] = (lo, lo + d)
        lo += d
    return out


def _token_tail(x, merged, mod, ng_ref, wout_ref, wq_ref, xmid_ref, h2_ref, q_ref, d):
    gate1 = mod[:, 2 * d:3 * d]
    shift2, scale2 = mod[:, 3 * d:4 * d], mod[:, 4 * d:5 * d]
    xm = x + gate1 * _dot(merged.astype(BF16), wout_ref[...])
    h2 = (_rms(xm, ng_ref[1:2, :]) * (1.0 + scale2) + shift2).astype(BF16)
    xmid_ref[0] = xm
    h2_ref[0] = h2
    q_ref[0] = _dot(h2, wq_ref[...]).astype(BF16)


def _k1_prompt_kernel(x_ref, mod_ref, ng_ref, win_ref, lnag_ref, lnab_ref, ws_ref, bs_ref, cbw_ref, ccw_ref,
                      ccb_ref, lncg_ref, lncb_ref, wbr_ref, wout_ref, wq_ref,
                      xmid_ref, h2_ref, q_ref, hb_ref, hc_ref,
                      ci_buf, glu_buf, yc_buf, mix_buf, *, tt, d):
    t = pl.program_id(1)
    dh = d // 2
    zs = _z_slices(d)
    taps_c = ccw_ref.shape[0]

    @pl.when(t == 0)
    def _():
        ci_buf[0:8, :] = jnp.zeros((8, dh), F32)
        glu_buf[0:32, :] = jnp.zeros((32, dh), F32)

    x = x_ref[0]
    mod = mod_ref[0]
    shift1, scale1 = mod[:, 0:d], mod[:, d:2 * d]
    hb16 = (_rms(x, ng_ref[0:1, :]) * (1.0 + scale1) + shift1).astype(BF16)

    def z(name):
        lo, hi = zs[name]
        return _dot(hb16, win_ref[:, lo:hi])

    ci = z("cb") * z("hb")
    ci_buf[8:8 + tt, :] = ci
    conv_b = (cbw_ref[0:1, :] * ci_buf[6:6 + tt, :] + cbw_ref[1:2, :] * ci_buf[7:7 + tt, :]
              + cbw_ref[2:3, :] * ci)
    yb = z("bb") * conv_b
    merged = _sigmoid(z("g1")) * _dot(yb.astype(BF16), wbr_ref[1])

    glu_buf[32:32 + tt, :] = z("ac") * _sigmoid(z("gc"))
    for r in range(tt // CONV_ROWS):
        base = r * CONV_ROWS + 32 - (taps_c - 1)
        acc = ccb_ref[...] + ccw_ref[0:1, :] * glu_buf[base:base + CONV_ROWS, :]
        for k in range(1, taps_c):
            acc = acc + ccw_ref[k:k + 1, :] * glu_buf[base + k:base + k + CONV_ROWS, :]
        y = _ln(acc, lncg_ref[...], lncb_ref[...])
        yc_buf[r * CONV_ROWS:(r + 1) * CONV_ROWS, :] = y * _sigmoid(y)
    merged = merged + _sigmoid(z("g2")) * _dot(yc_buf[...].astype(BF16), wbr_ref[2])

    chunk = ws_ref.shape[1]
    gdim = dh // ws_ref.shape[0]
    va = _ln(_gelu(z("va")), lnag_ref[...], lnab_ref[...]).astype(BF16)
    for c in range(tt // chunk):
        for g in range(ws_ref.shape[0]):
            mix_buf[c * chunk:(c + 1) * chunk, g * gdim:(g + 1) * gdim] = (
                _dot(ws_ref[g], va[c * chunk:(c + 1) * chunk, g * gdim:(g + 1) * gdim])
                + bs_ref[:, g * gdim:(g + 1) * gdim])
    ya = _gelu(z("ua")) * mix_buf[...]
    merged = merged + _sigmoid(z("g0")) * _dot(ya.astype(BF16), wbr_ref[0])

    _token_tail(x, merged, mod, ng_ref, wout_ref, wq_ref, xmid_ref, h2_ref, q_ref, d)

    @pl.when(t == pl.num_programs(1) - 1)
    def _():
        hb_ref[0] = ci_buf[tt + 6:tt + 8, :]
        hc_ref[0] = glu_buf[tt + 2:tt + 32, :]

    ci_buf[0:8, :] = ci_buf[tt:tt + 8, :]
    glu_buf[0:32, :] = glu_buf[tt:tt + 32, :]


def _k1_sample_kernel(x_ref, mod_ref, ng_ref, win_ref, lnag_ref, lnab_ref, w00_ref, b0_ref, cbw_ref, ccw_ref,
                      ccb_ref, lncg_ref, lncb_ref, wbr_ref, wout_ref, wq_ref, hb0_ref, hb1_ref, hc_hist_ref,
                      xmid_ref, h2_ref, q_ref, va_ref, ci_ref, glu_ref, *, d):
    zs = _z_slices(d)
    taps_c = ccw_ref.shape[0]
    x = x_ref[0]
    mod = mod_ref[0]
    shift1, scale1 = mod[:, 0:d], mod[:, d:2 * d]
    hb16 = (_rms(x, ng_ref[0:1, :]) * (1.0 + scale1) + shift1).astype(BF16)

    def z(name):
        lo, hi = zs[name]
        return _dot(hb16, win_ref[:, lo:hi])

    ci = z("cb") * z("hb")
    ci_ref[...] = ci
    conv_b = cbw_ref[0:1, :] * hb0_ref[...] + cbw_ref[1:2, :] * hb1_ref[...] + cbw_ref[2:3, :] * ci
    yb = z("bb") * conv_b
    merged = _sigmoid(z("g1")) * _dot(yb.astype(BF16), wbr_ref[1])

    glu = z("ac") * _sigmoid(z("gc"))
    glu_ref[...] = glu
    acc = ccb_ref[...] + ccw_ref[taps_c - 1:taps_c, :] * glu
    for k in range(taps_c - 1):
        acc = acc + ccw_ref[k:k + 1, :] * hc_hist_ref[k]
    y = _ln(acc, lncg_ref[...], lncb_ref[...])
    yc = y * _sigmoid(y)
    merged = merged + _sigmoid(z("g2")) * _dot(yc.astype(BF16), wbr_ref[2])

    va = _ln(_gelu(z("va")), lnag_ref[...], lnab_ref[...])
    va_ref[...] = va
    mixed = w00_ref[...].astype(F32) * va.astype(BF16).astype(F32) + b0_ref[...]
    ya = _gelu(z("ua")) * mixed
    merged = merged + _sigmoid(z("g0")) * _dot(ya.astype(BF16), wbr_ref[0])

    _token_tail(x, merged, mod, ng_ref, wout_ref, wq_ref, xmid_ref, h2_ref, q_ref, d)


def _k1_prompt_call(x, mod, lw, *, tt):
    b, t, d = x.shape
    dh = d // 2
    dq = lw["wq"].shape[1]
    kern = functools.partial(_k1_prompt_kernel, tt=tt, d=d)
    consts = [lw["ng"], lw["win"], lw["lnag"], lw["lnab"], lw["ws"], lw["bs"], lw["cbw"], lw["ccw"], lw["ccb"],
              lw["lncg"], lw["lncb"], lw["wbr"], lw["wout"], lw["wq"]]
    return pl.pallas_call(
        kern,
        grid=(b, t // tt),
        in_specs=[pl.BlockSpec((1, tt, d), lambda i, j: (i, j, 0)),
                  pl.BlockSpec((1, 1, mod.shape[-1]), lambda i, j: (i, 0, 0))]
        + [_const_spec(c.shape) for c in consts],
        out_specs=[pl.BlockSpec((1, tt, d), lambda i, j: (i, j, 0)),
                   pl.BlockSpec((1, tt, d), lambda i, j: (i, j, 0)),
                   pl.BlockSpec((1, tt, dq), lambda i, j: (i, j, 0)),
                   pl.BlockSpec((1, 2, dh), lambda i, j: (i, 0, 0)),
                   pl.BlockSpec((1, 30, dh), lambda i, j: (i, 0, 0))],
        out_shape=[jax.ShapeDtypeStruct((b, t, d), F32),
                   jax.ShapeDtypeStruct((b, t, d), BF16),
                   jax.ShapeDtypeStruct((b, t, dq), BF16),
                   jax.ShapeDtypeStruct((b, 2, dh), F32),
                   jax.ShapeDtypeStruct((b, 30, dh), F32)],
        scratch_shapes=[pltpu.VMEM((tt + 8, dh), F32), pltpu.VMEM((tt + 32, dh), F32),
                        pltpu.VMEM((tt, dh), F32), pltpu.VMEM((tt, dh), F32)],
        compiler_params=pltpu.CompilerParams(
            dimension_semantics=("arbitrary", "arbitrary"), vmem_limit_bytes=VMEM_LIMIT_BYTES),
        name="token_prompt",
    )(x, mod.reshape(b, 1, -1), *consts)


def _k1_sample_call(x, mod, lw, hb0, hb1, hc_hist):
    n, d = x.shape
    dh = d // 2
    dq = lw["wq"].shape[1]
    kern = functools.partial(_k1_sample_kernel, d=d)
    consts = [lw["ng"], lw["win"], lw["lnag"], lw["lnab"], lw["w00"], lw["b0"], lw["cbw"], lw["ccw"], lw["ccb"],
              lw["lncg"], lw["lncb"], lw["wbr"], lw["wout"], lw["wq"], hb0, hb1, hc_hist]
    row = lambda w: pl.BlockSpec((n, w), lambda i: (0, 0))
    return pl.pallas_call(
        kern,
        grid=(1,),
        in_specs=[pl.BlockSpec((1, n, d), lambda i: (0, 0, 0)),
                  pl.BlockSpec((1, n, mod.shape[-1]), lambda i: (0, 0, 0))]
        + [_const_spec(c.shape) for c in consts],
        out_specs=[pl.BlockSpec((1, n, d), lambda i: (0, 0, 0)),
                   pl.BlockSpec((1, n, d), lambda i: (0, 0, 0)),
                   pl.BlockSpec((1, n, dq), lambda i: (0, 0, 0)),
                   row(dh), row(dh), row(dh)],
        out_shape=[jax.ShapeDtypeStruct((1, n, d), F32),
                   jax.ShapeDtypeStruct((1, n, d), BF16),
                   jax.ShapeDtypeStruct((1, n, dq), BF16),
                   jax.ShapeDtypeStruct((n, dh), F32),
                   jax.ShapeDtypeStruct((n, dh), F32),
                   jax.ShapeDtypeStruct((n, dh), F32)],
        compiler_params=pltpu.CompilerParams(
            dimension_semantics=("arbitrary",), vmem_limit_bytes=VMEM_LIMIT_BYTES),
        name="token_sample",
    )(x[None], mod[None], *consts)


GELU_C0 = 0.7978845608028654
GELU_C1 = GELU_C0 * 0.044715
TAU_MARGIN = 2.0 ** -21


def _aligned(x, m):
    return x if isinstance(x, int) else pl.multiple_of(x, m)


def _oddeven_mergesort_pairs(n):
    pairs, p = [], 1
    while p < n:
        k = p
        while k >= 1:
            for j in range(k % p, n - k, 2 * k):
                for i in range(min(k, n - j - k)):
                    if (i + j) // (2 * p) == (i + j + k) // (2 * p):
                        pairs.append((i + j, i + j + k))
            k //= 2
        p *= 2
    return pairs


def _sort_desc(xs):
    xs = list(xs)
    for i, j in _oddeven_mergesort_pairs(len(xs)):
        xs[i], xs[j] = jnp.maximum(xs[i], xs[j]), jnp.minimum(xs[i], xs[j])
    return xs


def _merge_top(xs, ys):
    n = len(xs)
    zs = [jnp.maximum(xs[k], ys[n - 1 - k]) for k in range(n)]
    d = n // 2
    while d >= 1:
        for k in range(n):
            if k & d == 0:
                zs[k], zs[k + d] = jnp.maximum(zs[k], zs[k + d]), jnp.minimum(zs[k], zs[k + d])
        d //= 2
    return zs


def _peer_kernel(h2_ref, q_ref, x_ref, gate_ref, keys_ref, u_ref, vt_ref, fg_ref, out_ref,
                 th_s, s2_s, e1_s, e2_s, top_s, rowb_s, a0_s, a1_s, wa0_s, wa1_s, acc_s, *, ch, final, unit):
    e = pl.program_id(1)
    n_heads, n_chunks, n_keys, _ = th_s.shape
    rows = unit // n_keys
    rows_per_block = u_ref.shape[0] // n_keys
    upc = u_ref.shape[0] // unit
    n_units = n_chunks * upc
    n_pairs = n_units // 2
    assert upc % 2 == 0 and SUBLANES % rows == 0
    tiles_per_row = n_keys // SUBLANES
    assert tiles_per_row == PEER_TOPK
    neg_inf = jnp.float32(-jnp.inf)

    @pl.when(e == 0)
    def _scores_and_thresholds():
        acc_s[...] = jnp.zeros(acc_s.shape, F32)
        for c in range(n_chunks):
            qc = q_ref[c * ch:(c + 1) * ch, :]
            for h in range(n_heads):
                for p in range(2):
                    col = (2 * h + p) * n_keys
                    s = _nt_dot(keys_ref[h, p], qc[:, col:col + n_keys])
                    if p == 0:
                        th_s[h, c] = s
                    else:
                        s2_s[h, c] = s.reshape(tiles_per_row, SUBLANES, ch)
                    xs = _sort_desc([s[k * SUBLANES:(k + 1) * SUBLANES, :] for k in range(tiles_per_row)])
                    for shift in (4, 2, 1):
                        xs = _merge_top(xs, [pltpu.roll(v, shift, 0) for v in xs])
                    for r in range(PEER_TOPK):
                        top_s[p, r, h:h + 1, :] = xs[r][0:1, :]
            a = [top_s[0, r] for r in range(PEER_TOPK)]
            b = [top_s[1, r] for r in range(PEER_TOPK)]
            pad = jnp.full(a[0].shape, neg_inf, F32)
            lists = [[a[r] + b[s] for r in range(PEER_TOPK // (s + 1))] for s in range(PEER_TOPK // 2)]
            lists.append([a[0] + b[s] for s in range(PEER_TOPK // 2, PEER_TOPK)])
            lists = [l + [pad] * (PEER_TOPK - len(l)) for l in lists]
            best = functools.reduce(_merge_top, lists)
            top = best[0]
            tau = best[PEER_TOPK - 1] - jnp.abs(best[PEER_TOPK - 1]) * TAU_MARGIN
            zsum = functools.reduce(lambda acc, v: acc + jnp.exp(v - top), best[1:], jnp.ones_like(top))
            half_zinv = 0.5 / zsum
            for h in range(n_heads):
                hs = slice(h, h + 1)
                s1 = th_s[h, c]
                e1_s[h, c] = jnp.exp(s1 - a[0][hs, :])
                th_s[h, c] = tau[hs, :] - s1
                e2_s[h, c] = jnp.exp(s2_s[h, c] - b[0][hs, :]) * half_zinv[hs, :]

    pairs_per_chunk = upc // 2
    a_bufs = (a0_s, a1_s)
    wa_bufs = (wa0_s, wa1_s)

    def stage_a(c):
        prod = _nt_dot(u_ref[...], h2_ref[c * ch:(c + 1) * ch, :])
        a_bufs[c % 2][...] = prod.reshape(u_ref.shape[0] // SUBLANES, SUBLANES, ch)

    def stage_b(c, k, wa_s, half):
        a_s = a_bufs[c % 2]
        i8 = pl.multiple_of(e * rows_per_block + (k * rows // SUBLANES) * SUBLANES, SUBLANES)
        to_top = (SUBLANES - (k * rows) % SUBLANES) % SUBLANES
        ub = k % 2
        for h in range(n_heads):
            for which, ref in enumerate((th_s, e1_s)):
                tile = ref[h, c, pl.ds(i8, SUBLANES), :]
                if to_top:
                    tile = pltpu.roll(tile, to_top, 0)
                for r in range(rows):
                    rowb_s[ub, which, r, h] = jnp.broadcast_to(tile[r:r + 1, :], (SUBLANES, ch))
        for lt in range(ch // LANES):
            ls = slice(lt * LANES, (lt + 1) * LANES)
            for jq in range(tiles_per_row // PEER_J_TILES):
                js = slice(jq * PEER_J_TILES, (jq + 1) * PEER_J_TILES)
                wsum = [None] * rows
                for h in range(n_heads):
                    s2t = s2_s[h, c, js, :, ls]
                    e2t = e2_s[h, c, js, :, ls]
                    for r in range(rows):
                        keep = s2t >= rowb_s[ub, 0, r, h, :, ls]
                        term = rowb_s[ub, 1, r, h, :, ls] * jnp.where(keep, e2t, 0.0)
                        wsum[r] = term if wsum[r] is None else wsum[r] + term
                for r in range(rows):
                    t0 = (k * rows + r) * tiles_per_row + jq * PEER_J_TILES
                    av = a_s[t0:t0 + PEER_J_TILES, :, ls]
                    y = av * (1.0 + jnp.tanh(av * (GELU_C0 + GELU_C1 * (av * av))))
                    r0 = half * unit + (r * tiles_per_row + jq * PEER_J_TILES) * SUBLANES
                    wa_s[r0:r0 + PEER_J_TILES * SUBLANES, ls] = (
                        (wsum[r] * y).reshape(PEER_J_TILES * SUBLANES, LANES).astype(BF16))

    def stage_c(c, kp, wa_s):
        acc_s[c] = acc_s[c] + _dot(vt_ref[kp], wa_s[...])

    stage_a(0)
    prev = None
    for c in range(n_chunks):
        for kp in range(pairs_per_chunk):
            def pair_block(c=c, kp=kp, prev=prev):
                wa_s = wa_bufs[(c * pairs_per_chunk + kp) % 2]
                if kp == 0 and c + 1 < n_chunks:
                    stage_a(c + 1)
                if prev is not None:
                    stage_c(*prev)
                stage_b(c, 2 * kp, wa_s, 0)
                stage_b(c, 2 * kp + 1, wa_s, 1)

            pl.when(e >= 0)(pair_block)
            prev = (c, kp, wa_bufs[(c * pairs_per_chunk + kp) % 2])
    stage_c(*prev)

    @pl.when(e == pl.num_programs(1) - 1)
    def _epilogue():
        for c in range(n_chunks):
            tok = slice(c * ch, (c + 1) * ch)
            y = x_ref[tok, :] + gate_ref[0] * acc_s[c].T
            if final:
                y = _rms(y, fg_ref[...])
            out_ref[tok, :] = y


def _peer_call(h2, q, x, gate, keys, u, vt, fg, *, tile, eb, final):
    n, d = x.shape
    n_heads, _, n_keys, _ = keys.shape
    n_exp = u.shape[0]
    pair = vt.shape[2]
    unit = pair // 2
    ch = min(MXU_COLS, tile)
    n_chunks = tile // ch
    groups, grows, _ = gate.shape
    tiles_per_group = (n // tile) // groups
    kern = functools.partial(_peer_kernel, ch=ch, final=final, unit=unit)
    row_scratch = pltpu.VMEM((n_heads, n_chunks, n_keys, ch), F32)
    tile_scratch = pltpu.VMEM((n_heads, n_chunks, n_keys // SUBLANES, SUBLANES, ch), F32)
    prod_scratch = pltpu.VMEM((eb // SUBLANES, SUBLANES, ch), F32)
    return pl.pallas_call(
        kern,
        grid=(n // tile, n_exp // eb),
        in_specs=[pl.BlockSpec((tile, d), lambda t, e: (t, 0)),
                  pl.BlockSpec((tile, q.shape[1]), lambda t, e: (t, 0)),
                  pl.BlockSpec((tile, d), lambda t, e: (t, 0)),
                  pl.BlockSpec((1, grows, d), lambda t, e: (t // tiles_per_group, 0, 0)),
                  _const_spec(keys.shape),
                  pl.BlockSpec((eb, d), lambda t, e: (e, 0)),
                  pl.BlockSpec((eb // pair, d, pair), lambda t, e: (e, 0, 0)),
                  _const_spec(fg.shape)],
        out_specs=pl.BlockSpec((tile, d), lambda t, e: (t, 0)),
        out_shape=jax.ShapeDtypeStruct((n, d), F32),
        scratch_shapes=[row_scratch, tile_scratch, row_scratch, tile_scratch,
                        pltpu.VMEM((2, PEER_TOPK, n_heads, ch), F32),
                        pltpu.VMEM((2, 2, unit // n_keys, n_heads, SUBLANES, ch), F32),
                        prod_scratch, prod_scratch,
                        pltpu.VMEM((pair, ch), BF16), pltpu.VMEM((pair, ch), BF16),
                        pltpu.VMEM((n_chunks, d, ch), F32)],
        compiler_params=pltpu.CompilerParams(
            dimension_semantics=("arbitrary", "arbitrary"), vmem_limit_bytes=VMEM_LIMIT_BYTES),
        name="peer_final" if final else "peer",
    )(h2, q, x, gate, keys, u, vt, fg)


TOKEN_TILE = 256
PEER_TILE = 512
PEER_EXPERT_BLOCK = 2048
PEER_UNIT_ROWS = 4
PEER_J_TILES = 4


def kernel(x_prompt, x_sample, c_prompt, c_sample, state_conv_b, state_conv_c, norm_g, final_g, ada_w, ada_b, w_in, ln_a_g, ln_a_b, w_spatial, b_spatial, conv_b_w, conv_c_w, conv_c_b, ln_c_g, ln_c_b, w_branch, w_out, peer_w_query, peer_sub_keys, peer_u, peer_v):
    depth = w_in.shape[0]
    bp, seq, d = x_prompt.shape
    ns = x_sample.shape[0]
    dh = d // 2
    groups, chunk = w_spatial.shape[1], w_spatial.shape[2]
    gdim = dh // groups

    mod_all = _mod_call(jnp.concatenate([c_prompt, c_sample], axis=0), ada_w, ada_b)

    tril = jnp.tril(jnp.ones((chunk, chunk), dtype=bool))
    ws_all = jnp.where(tril[None, None], w_spatial, 0.0).astype(BF16)
    bs_all = jnp.repeat(jnp.swapaxes(b_spatial, 1, 2), gdim, axis=2)
    w00_all = jnp.repeat(w_spatial[:, :, 0, 0], gdim, axis=1).astype(BF16)[:, None, :]
    b0_all = jnp.repeat(b_spatial[:, :, 0], gdim, axis=1)[:, None, :]
    win_all = w_in.astype(BF16)
    wbr_all = w_branch.astype(BF16)
    wout_all = w_out.astype(BF16)
    wq_all = peer_w_query.astype(BF16)
    keys_all = peer_sub_keys.astype(BF16)
    u_all = peer_u.astype(BF16)
    n_exp = peer_u.shape[1]
    pair = 2 * PEER_UNIT_ROWS * peer_sub_keys.shape[3]
    vt_all = jnp.swapaxes(peer_v.reshape(depth, n_exp // pair, pair, d), 2, 3).astype(BF16)
    hc_hist_all = jnp.swapaxes(state_conv_c, 1, 2)
    fg = final_g[None, :]

    xp = x_prompt
    xs = x_sample.reshape(ns, d)
    v_s, hb_p, hb_s, hc_p, hc_s = [], [], [], [], []
    for l in range(depth):
        lw = dict(ng=norm_g[l], win=win_all[l], lnag=ln_a_g[l][None], lnab=ln_a_b[l][None], ws=ws_all[l],
                  bs=bs_all[l], w00=w00_all[l], b0=b0_all[l], cbw=conv_b_w[l], ccw=conv_c_w[l],
                  ccb=conv_c_b[l][None], lncg=ln_c_g[l][None], lncb=ln_c_b[l][None], wbr=wbr_all[l],
                  wout=wout_all[l], wq=wq_all[l])
        mod_p, mod_s = mod_all[l, :bp], mod_all[l, bp:]
        final = l == depth - 1

        xmid, h2, q, hb, hc = _k1_prompt_call(xp, mod_p, lw, tt=TOKEN_TILE)
        hb_p.append(hb)
        hc_p.append(hc)
        xp = _peer_call(h2.reshape(bp * seq, d), q.reshape(bp * seq, -1), xmid.reshape(bp * seq, d),
                        mod_p[:, None, 5 * d:], keys_all[l], u_all[l], vt_all[l], fg,
                        tile=PEER_TILE, eb=PEER_EXPERT_BLOCK, final=final).reshape(bp, seq, d)

        xmid, h2, q, va, ci, glu = _k1_sample_call(xs, mod_s, lw, state_conv_b[l, :, 0], state_conv_b[l, :, 1],
                                                   hc_hist_all[l])
        v_s.append(va[:, None, :])
        hb_s.append(jnp.stack([state_conv_b[l, :, 1], ci], axis=1))
        hc_s.append(jnp.concatenate([state_conv_c[l, :, 1:], glu[:, None, :]], axis=1))
        xs = _peer_call(h2[0], q[0], xmid[0], mod_s[None, :, 5 * d:], keys_all[l], u_all[l], vt_all[l], fg,
                        tile=ns, eb=PEER_EXPERT_BLOCK, final=final)

    return (xp, xs.reshape(ns, 1, d), jnp.stack(v_s), jnp.stack(hb_p), jnp.stack(hb_s), jnp.stack(hc_p),
            jnp.stack(hc_s))
```

```python
import functools

import jax
import jax.numpy as jnp
from jax import lax
from jax.experimental import pallas as pl
from jax.experimental.pallas import tpu as pltpu

F32 = jnp.float32
BF16 = jnp.bfloat16
EPS = 1e-6

VMEM_LIMIT_BYTES = 56 * 1024 * 1024
LANES = 128
SUBLANES = 8
MXU_COLS = 256

PEER_TOPK = 16
CONV_ROWS = 32


def _nt_dot(a, b):
    return lax.dot_general(a, b, (((1,), (1,)), ((), ())), preferred_element_type=F32)


def _dot(a, b):
    return jnp.dot(a, b, preferred_element_type=F32)


def _rms(x, g):
    return x * lax.rsqrt(jnp.mean(x * x, axis=-1, keepdims=True) + EPS) * g


def _ln(x, g, b):
    mu = jnp.mean(x, axis=-1, keepdims=True)
    xc = x - mu
    var = jnp.mean(xc * xc, axis=-1, keepdims=True)
    return xc * lax.rsqrt(var + EPS) * g + b


def _gelu(x):
    return 0.5 * x * (1.0 + jnp.tanh(0.7978845608028654 * (x + 0.044715 * (x * x * x))))


def _sigmoid(x):
    return 1.0 / (1.0 + jnp.exp(-x))


def _const_spec(shape):
    nd = len(shape)
    return pl.BlockSpec(shape, lambda *_: (0,) * nd, pipeline_mode=pl.Buffered(1))


class _Stacked:
    def __init__(self, array):
        self.array = array
        self.shape = array.shape[1:]


def _weight_spec(w, layer):
    return _layer_spec(w.array.shape, layer) if isinstance(w, _Stacked) else _const_spec(w.shape)


def _unstack(ws):
    return [w.array if isinstance(w, _Stacked) else w for w in ws]


def _layer_spec(stacked_shape, layer):
    nd = len(stacked_shape)
    return pl.BlockSpec((None,) + tuple(stacked_shape[1:]), lambda *_: (layer,) + (0,) * (nd - 1),
                        pipeline_mode=pl.Buffered(1))


def _mod_kernel(c_ref, w_ref, b_ref, o_ref):
    c = c_ref[...]
    s = (c * _sigmoid(c)).astype(BF16)
    o_ref[0] = _dot(s, w_ref[0].astype(BF16)) + b_ref[0]


def _mod_call(c_all, ada_w, ada_b):
    depth, d, n = ada_w.shape
    m = c_all.shape[0]
    tn = 1536
    return pl.pallas_call(
        _mod_kernel,
        grid=(depth, n // tn),
        in_specs=[
            pl.BlockSpec((m, d), lambda l, j: (0, 0)),
            pl.BlockSpec((1, d, tn), lambda l, j: (l, 0, j)),
            pl.BlockSpec((1, 1, tn), lambda l, j: (l, 0, j)),
        ],
        out_specs=pl.BlockSpec((1, m, tn), lambda l, j: (l, 0, j)),
        out_shape=jax.ShapeDtypeStruct((depth, m, n), F32),
        compiler_params=pltpu.CompilerParams(
            dimension_semantics=("arbitrary", "arbitrary"), vmem_limit_bytes=VMEM_LIMIT_BYTES),
        name="adaln_mod",
    )(c_all, ada_w, ada_b.reshape(depth, 1, n))


def _z_slices(d):
    h = d // 2
    names = ["ua", "va", "hb", "bb", "cb", "ac", "gc"]
    out, lo = {}, 0
    for n in names:
        out[n] = (lo, lo + h)
        lo += h
    for k in range(3):
        out[f"g{k}"] = (lo, lo + d)
        lo += d
    return out


def _token_tail(x, merged, mod, ng_ref, wout_ref, wq_ref, xmid_ref, h2_ref, q_ref, d):
    gate1 = mod[:, 2 * d:3 * d]
    shift2, scale2 = mod[:, 3 * d:4 * d], mod[:, 4 * d:5 * d]
    xm = x + gate1 * _dot(merged.astype(BF16), wout_ref[...])
    h2 = (_rms(xm, ng_ref[1:2, :]) * (1.0 + scale2) + shift2).astype(BF16)
    xmid_ref[0] = xm
    h2_ref[0] = h2
    q_ref[0] = _dot(h2, wq_ref[...]).astype(BF16)


def _k1_prompt_kernel(x_ref, mod_ref, ng_ref, win_ref, lnag_ref, lnab_ref, ws_ref, bs_ref, cbw_ref, ccw_ref,
                      ccb_ref, lncg_ref, lncb_ref, wbr_ref, wout_ref, wq_ref,
                      xmid_ref, h2_ref, q_ref, hb_ref, hc_ref,
                      ci_buf, glu_buf, yc_buf, mix_buf, *, tt, d):
    t = pl.program_id(1)
    dh = d // 2
    zs = _z_slices(d)
    taps_c = ccw_ref.shape[0]

    @pl.when(t == 0)
    def _():
        ci_buf[0:8, :] = jnp.zeros((8, dh), F32)
        glu_buf[0:32, :] = jnp.zeros((32, dh), F32)

    x = x_ref[0]
    mod = mod_ref[0]
    shift1, scale1 = mod[:, 0:d], mod[:, d:2 * d]
    hb16 = (_rms(x, ng_ref[0:1, :]) * (1.0 + scale1) + shift1).astype(BF16)

    def z(name):
        lo, hi = zs[name]
        return _dot(hb16, win_ref[:, lo:hi])

    ci = z("cb") * z("hb")
    ci_buf[8:8 + tt, :] = ci
    conv_b = (cbw_ref[0:1, :] * ci_buf[6:6 + tt, :] + cbw_ref[1:2, :] * ci_buf[7:7 + tt, :]
              + cbw_ref[2:3, :] * ci)
    yb = z("bb") * conv_b
    merged = _sigmoid(z("g1")) * _dot(yb.astype(BF16), wbr_ref[1])

    glu_buf[32:32 + tt, :] = z("ac") * _sigmoid(z("gc"))
    for r in range(tt // CONV_ROWS):
        base = r * CONV_ROWS + 32 - (taps_c - 1)
        acc = ccb_ref[...] + ccw_ref[0:1, :] * glu_buf[base:base + CONV_ROWS, :]
        for k in range(1, taps_c):
            acc = acc + ccw_ref[k:k + 1, :] * glu_buf[base + k:base + k + CONV_ROWS, :]
        y = _ln(acc, lncg_ref[...], lncb_ref[...])
        yc_buf[r * CONV_ROWS:(r + 1) * CONV_ROWS, :] = y * _sigmoid(y)
    merged = merged + _sigmoid(z("g2")) * _dot(yc_buf[...].astype(BF16), wbr_ref[2])

    chunk = ws_ref.shape[1]
    gdim = dh // ws_ref.shape[0]
    va = _ln(_gelu(z("va")), lnag_ref[...], lnab_ref[...]).astype(BF16)
    for c in range(tt // chunk):
        for g in range(ws_ref.shape[0]):
            mix_buf[c * chunk:(c + 1) * chunk, g * gdim:(g + 1) * gdim] = (
                _dot(ws_ref[g], va[c * chunk:(c + 1) * chunk, g * gdim:(g + 1) * gdim])
                + bs_ref[:, g * gdim:(g + 1) * gdim])
    ya = _gelu(z("ua")) * mix_buf[...]
    merged = merged + _sigmoid(z("g0")) * _dot(ya.astype(BF16), wbr_ref[0])

    _token_tail(x, merged, mod, ng_ref, wout_ref, wq_ref, xmid_ref, h2_ref, q_ref, d)

    @pl.when(t == pl.num_programs(1) - 1)
    def _():
        hb_ref[0] = ci_buf[tt + 6:tt + 8, :]
        hc_ref[0] = glu_buf[tt + 2:tt + 32, :]

    ci_buf[0:8, :] = ci_buf[tt:tt + 8, :]
    glu_buf[0:32, :] = glu_buf[tt:tt + 32, :]


def _k1_sample_kernel(x_ref, mod_ref, ng_ref, win_ref, lnag_ref, lnab_ref, w00_ref, b0_ref, cbw_ref, ccw_ref,
                      ccb_ref, lncg_ref, lncb_ref, wbr_ref, wout_ref, wq_ref, hb0_ref, hb1_ref, hc_hist_ref,
                      xmid_ref, h2_ref, q_ref, va_ref, ci_ref, glu_ref, *, d):
    zs = _z_slices(d)
    taps_c = ccw_ref.shape[0]
    x = x_ref[0]
    mod = mod_ref[0]
    shift1, scale1 = mod[:, 0:d], mod[:, d:2 * d]
    hb16 = (_rms(x, ng_ref[0:1, :]) * (1.0 + scale1) + shift1).astype(BF16)

    def z(name):
        lo, hi = zs[name]
        return _dot(hb16, win_ref[:, lo:hi])

    ci = z("cb") * z("hb")
    ci_ref[...] = ci
    conv_b = cbw_ref[0:1, :] * hb0_ref[...] + cbw_ref[1:2, :] * hb1_ref[...] + cbw_ref[2:3, :] * ci
    yb = z("bb") * conv_b
    merged = _sigmoid(z("g1")) * _dot(yb.astype(BF16), wbr_ref[1])

    glu = z("ac") * _sigmoid(z("gc"))
    glu_ref[...] = glu
    acc = ccb_ref[...] + ccw_ref[taps_c - 1:taps_c, :] * glu
    for k in range(taps_c - 1):
        acc = acc + ccw_ref[k:k + 1, :] * hc_hist_ref[k]
    y = _ln(acc, lncg_ref[...], lncb_ref[...])
    yc = y * _sigmoid(y)
    merged = merged + _sigmoid(z("g2")) * _dot(yc.astype(BF16), wbr_ref[2])

    va = _ln(_gelu(z("va")), lnag_ref[...], lnab_ref[...])
    va_ref[...] = va
    mixed = w00_ref[...].astype(F32) * va.astype(BF16).astype(F32) + b0_ref[...]
    ya = _gelu(z("ua")) * mixed
    merged = merged + _sigmoid(z("g0")) * _dot(ya.astype(BF16), wbr_ref[0])

    _token_tail(x, merged, mod, ng_ref, wout_ref, wq_ref, xmid_ref, h2_ref, q_ref, d)


def _k1_prompt_call(x, mod, lw, *, tt):
    b, t, d = x.shape
    dh = d // 2
    dq = lw["wq"].shape[-1]
    kern = functools.partial(_k1_prompt_kernel, tt=tt, d=d)
    consts = [lw["ng"], lw["win"], lw["lnag"], lw["lnab"], lw["ws"], lw["bs"], lw["cbw"], lw["ccw"], lw["ccb"],
              lw["lncg"], lw["lncb"], lw["wbr"], lw["wout"], lw["wq"]]
    return pl.pallas_call(
        kern,
        grid=(b, t // tt),
        in_specs=[pl.BlockSpec((1, tt, d), lambda i, j: (i, j, 0)),
                  pl.BlockSpec((1, 1, mod.shape[-1]), lambda i, j: (i, 0, 0))]
        + [_weight_spec(c, lw["layer"]) for c in consts],
        out_specs=[pl.BlockSpec((1, tt, d), lambda i, j: (i, j, 0)),
                   pl.BlockSpec((1, tt, d), lambda i, j: (i, j, 0)),
                   pl.BlockSpec((1, tt, dq), lambda i, j: (i, j, 0)),
                   pl.BlockSpec((1, 2, dh), lambda i, j: (i, 0, 0)),
                   pl.BlockSpec((1, 30, dh), lambda i, j: (i, 0, 0))],
        out_shape=[jax.ShapeDtypeStruct((b, t, d), F32),
                   jax.ShapeDtypeStruct((b, t, d), BF16),
                   jax.ShapeDtypeStruct((b, t, dq), BF16),
                   jax.ShapeDtypeStruct((b, 2, dh), F32),
                   jax.ShapeDtypeStruct((b, 30, dh), F32)],
        scratch_shapes=[pltpu.VMEM((tt + 8, dh), F32), pltpu.VMEM((tt + 32, dh), F32),
                        pltpu.VMEM((tt, dh), F32), pltpu.VMEM((tt, dh), F32)],
        compiler_params=pltpu.CompilerParams(
            dimension_semantics=("arbitrary", "arbitrary"), vmem_limit_bytes=VMEM_LIMIT_BYTES),
        name="token_prompt",
    )(x, mod.reshape(b, 1, -1), *_unstack(consts))


def _k1_sample_call(x, mod, lw, hb0, hb1, hc_hist):
    n, d = x.shape
    dh = d // 2
    dq = lw["wq"].shape[-1]
    kern = functools.partial(_k1_sample_kernel, d=d)
    consts = [lw["ng"], lw["win"], lw["lnag"], lw["lnab"], lw["w00"], lw["b0"], lw["cbw"], lw["ccw"], lw["ccb"],
              lw["lncg"], lw["lncb"], lw["wbr"], lw["wout"], lw["wq"], hb0, hb1, hc_hist]
    row = lambda w: pl.BlockSpec((n, w), lambda i: (0, 0))
    return pl.pallas_call(
        kern,
        grid=(1,),
        in_specs=[pl.BlockSpec((1, n, d), lambda i: (0, 0, 0)),
                  pl.BlockSpec((1, n, mod.shape[-1]), lambda i: (0, 0, 0))]
        + [_weight_spec(c, lw["layer"]) for c in consts],
        out_specs=[pl.BlockSpec((1, n, d), lambda i: (0, 0, 0)),
                   pl.BlockSpec((1, n, d), lambda i: (0, 0, 0)),
                   pl.BlockSpec((1, n, dq), lambda i: (0, 0, 0)),
                   row(dh), row(dh), row(dh)],
        out_shape=[jax.ShapeDtypeStruct((1, n, d), F32),
                   jax.ShapeDtypeStruct((1, n, d), BF16),
                   jax.ShapeDtypeStruct((1, n, dq), BF16),
                   jax.ShapeDtypeStruct((n, dh), F32),
                   jax.ShapeDtypeStruct((n, dh), F32),
                   jax.ShapeDtypeStruct((n, dh), F32)],
        compiler_params=pltpu.CompilerParams(
            dimension_semantics=("arbitrary",), vmem_limit_bytes=VMEM_LIMIT_BYTES),
        name="token_sample",
    )(x[None], mod[None], *_unstack(consts))


GELU_C0 = 0.7978845608028654
GELU_C1 = GELU_C0 * 0.044715
TAU_MARGIN = 2.0 ** -21


def _aligned(x, m):
    return x if isinstance(x, int) else pl.multiple_of(x, m)


def _oddeven_mergesort_pairs(n):
    pairs, p = [], 1
    while p < n:
        k = p
        while k >= 1:
            for j in range(k % p, n - k, 2 * k):
                for i in range(min(k, n - j - k)):
                    if (i + j) // (2 * p) == (i + j + k) // (2 * p):
                        pairs.append((i + j, i + j + k))
            k //= 2
        p *= 2
    return pairs


def _sort_desc(xs):
    xs = list(xs)
    for i, j in _oddeven_mergesort_pairs(len(xs)):
        xs[i], xs[j] = jnp.maximum(xs[i], xs[j]), jnp.minimum(xs[i], xs[j])
    return xs


def _merge_top(xs, ys):
    n = len(xs)
    zs = [jnp.maximum(xs[k], ys[n - 1 - k]) for k in range(n)]
    d = n // 2
    while d >= 1:
        for k in range(n):
            if k & d == 0:
                zs[k], zs[k + d] = jnp.maximum(zs[k], zs[k + d]), jnp.minimum(zs[k], zs[k + d])
        d //= 2
    return zs


def _peer_kernel(h2_ref, q_ref, x_ref, gate_ref, keys_ref, u_ref, vt_ref, fg_ref, out_ref,
                 th_s, s2_s, e1_s, e2_s, top_s, rowb_s, a0_s, a1_s, wa0_s, wa1_s, wa2_s, wa3_s, acc_s,
                 *, ch, final, unit):
    e = pl.program_id(1)
    n_heads, n_chunks, n_keys, _ = th_s.shape
    rows = unit // n_keys
    rows_per_block = u_ref.shape[0] // n_keys
    upc = u_ref.shape[0] // unit
    n_units = n_chunks * upc
    n_pairs = n_units // 2
    assert upc % 2 == 0 and SUBLANES % rows == 0
    tiles_per_row = n_keys // SUBLANES
    assert tiles_per_row == PEER_TOPK
    neg_inf = jnp.float32(-jnp.inf)

    @pl.when(e == 0)
    def _scores_and_thresholds():
        acc_s[...] = jnp.zeros(acc_s.shape, F32)
        for c in range(n_chunks):
            qc = q_ref[c * ch:(c + 1) * ch, :]
            for h in range(n_heads):
                for p in range(2):
                    col = (2 * h + p) * n_keys
                    s = _nt_dot(keys_ref[h, p], qc[:, col:col + n_keys])
                    if p == 0:
                        th_s[h, c] = s
                    else:
                        s2_s[h, c] = s.reshape(tiles_per_row, SUBLANES, ch)
                    xs = _sort_desc([s[k * SUBLANES:(k + 1) * SUBLANES, :] for k in range(tiles_per_row)])
                    for shift in (4, 2, 1):
                        xs = _merge_top(xs, [pltpu.roll(v, shift, 0) for v in xs])
                    for r in range(PEER_TOPK):
                        top_s[p, r, h:h + 1, :] = xs[r][0:1, :]
            a = [top_s[0, r] for r in range(PEER_TOPK)]
            b = [top_s[1, r] for r in range(PEER_TOPK)]
            pad = jnp.full(a[0].shape, neg_inf, F32)
            lists = [[a[r] + b[s] for r in range(PEER_TOPK // (s + 1))] for s in range(PEER_TOPK // 2)]
            lists.append([a[0] + b[s] for s in range(PEER_TOPK // 2, PEER_TOPK)])
            lists = [l + [pad] * (PEER_TOPK - len(l)) for l in lists]
            best = functools.reduce(_merge_top, lists)
            top = best[0]
            tau = best[PEER_TOPK - 1] - jnp.abs(best[PEER_TOPK - 1]) * TAU_MARGIN
            zsum = functools.reduce(lambda acc, v: acc + jnp.exp(v - top), best[1:], jnp.ones_like(top))
            half_zinv = 0.5 / zsum
            for h in range(n_heads):
                hs = slice(h, h + 1)
                s1 = th_s[h, c]
                e1_s[h, c] = jnp.exp(s1 - a[0][hs, :])
                th_s[h, c] = tau[hs, :] - s1
                e2_s[h, c] = jnp.exp(s2_s[h, c] - b[0][hs, :]) * half_zinv[hs, :]

    pairs_per_chunk = upc // 2
    a_bufs = (a0_s, a1_s)
    wa_bufs = (wa0_s, wa1_s, wa2_s, wa3_s)

    def stage_a(c):
        prod = _nt_dot(u_ref[...], h2_ref[c * ch:(c + 1) * ch, :])
        a_bufs[c % 2][...] = prod.reshape(u_ref.shape[0] // SUBLANES, SUBLANES, ch)

    def stage_b(c, k, wa_s, half):
        a_s = a_bufs[c % 2]
        i8 = pl.multiple_of(e * rows_per_block + (k * rows // SUBLANES) * SUBLANES, SUBLANES)
        to_top = (SUBLANES - (k * rows) % SUBLANES) % SUBLANES
        ub = k % rowb_s.shape[0]
        for h in range(n_heads):
            for which, ref in enumerate((th_s, e1_s)):
                tile = ref[h, c, pl.ds(i8, SUBLANES), :]
                if to_top:
                    tile = pltpu.roll(tile, to_top, 0)
                for r in range(rows):
                    rowb_s[ub, which, r, h] = jnp.broadcast_to(tile[r:r + 1, :], (SUBLANES, ch))
        for lt in range(ch // LANES):
            ls = slice(lt * LANES, (lt + 1) * LANES)
            for jq in range(tiles_per_row // PEER_J_TILES):
                js = slice(jq * PEER_J_TILES, (jq + 1) * PEER_J_TILES)
                wsum = [None] * rows
                for h in range(n_heads):
                    s2t = s2_s[h, c, js, :, ls]
                    e2t = e2_s[h, c, js, :, ls]
                    for r in range(rows):
                        keep = s2t >= rowb_s[ub, 0, r, h, :, ls]
                        term = rowb_s[ub, 1, r, h, :, ls] * jnp.where(keep, e2t, 0.0)
                        wsum[r] = term if wsum[r] is None else wsum[r] + term
                for r in range(rows):
                    t0 = (k * rows + r) * tiles_per_row + jq * PEER_J_TILES
                    av = a_s[t0:t0 + PEER_J_TILES, :, ls]
                    y = av * (1.0 + jnp.tanh(av * (GELU_C0 + GELU_C1 * (av * av))))
                    r0 = half * unit + (r * tiles_per_row + jq * PEER_J_TILES) * SUBLANES
                    wa_s[r0:r0 + PEER_J_TILES * SUBLANES, ls] = (
                        (wsum[r] * y).reshape(PEER_J_TILES * SUBLANES, LANES).astype(BF16))

    def stage_c(c, kp, wa_s):
        acc_s[c] = acc_s[c] + _dot(vt_ref[kp], wa_s[...])

    stage_a(0)
    prev = None
    for c in range(n_chunks):
        def chunk_block(c=c, prev=prev):
            for kp in range(pairs_per_chunk):
                wa_s = wa_bufs[(c * pairs_per_chunk + kp) % len(wa_bufs)]
                if kp == 0 and c + 1 < n_chunks:
                    stage_a(c + 1)
                if prev is not None:
                    stage_c(*prev)
                stage_b(c, 2 * kp, wa_s, 0)
                stage_b(c, 2 * kp + 1, wa_s, 1)
                prev = (c, kp, wa_s)

        pl.when(e >= 0)(chunk_block)
        last = (c + 1) * pairs_per_chunk - 1
        prev = (c, pairs_per_chunk - 1, wa_bufs[last % len(wa_bufs)])
    stage_c(*prev)

    @pl.when(e == pl.num_programs(1) - 1)
    def _epilogue():
        for c in range(n_chunks):
            tok = slice(c * ch, (c + 1) * ch)
            y = x_ref[tok, :] + gate_ref[0] * acc_s[c].T
            if final:
                y = _rms(y, fg_ref[...])
            out_ref[tok, :] = y


def _peer_call(h2, q, x, gate, keys, u, vt, fg, *, layer, tile, eb, final):
    n, d = x.shape
    _, n_heads, _, n_keys, _ = keys.shape
    n_exp = u.shape[1]
    pair = vt.shape[3]
    unit = pair // 2
    ch = min(MXU_COLS, tile)
    n_chunks = tile // ch
    groups, grows, _ = gate.shape
    tiles_per_group = (n // tile) // groups
    kern = functools.partial(_peer_kernel, ch=ch, final=final, unit=unit)
    row_scratch = pltpu.VMEM((n_heads, n_chunks, n_keys, ch), F32)
    tile_scratch = pltpu.VMEM((n_heads, n_chunks, n_keys // SUBLANES, SUBLANES, ch), F32)
    prod_scratch = pltpu.VMEM((eb // SUBLANES, SUBLANES, ch), F32)
    return pl.pallas_call(
        kern,
        grid=(n // tile, n_exp // eb),
        in_specs=[pl.BlockSpec((tile, d), lambda t, e: (t, 0)),
                  pl.BlockSpec((tile, q.shape[1]), lambda t, e: (t, 0)),
                  pl.BlockSpec((tile, d), lambda t, e: (t, 0)),
                  pl.BlockSpec((1, grows, d), lambda t, e: (t // tiles_per_group, 0, 0)),
                  _layer_spec(keys.shape, layer),
                  pl.BlockSpec((None, eb, d), lambda t, e: (layer, e, 0)),
                  pl.BlockSpec((None, eb // pair, d, pair), lambda t, e: (layer, e, 0, 0)),
                  _const_spec(fg.shape)],
        out_specs=pl.BlockSpec((tile, d), lambda t, e: (t, 0)),
        out_shape=jax.ShapeDtypeStruct((n, d), F32),
        scratch_shapes=[row_scratch, tile_scratch, row_scratch, tile_scratch,
                        pltpu.VMEM((2, PEER_TOPK, n_heads, ch), F32),
                        pltpu.VMEM((eb // unit, 2, unit // n_keys, n_heads, SUBLANES, ch), F32),
                        prod_scratch, prod_scratch,
                        pltpu.VMEM((pair, ch), BF16), pltpu.VMEM((pair, ch), BF16),
                        pltpu.VMEM((pair, ch), BF16), pltpu.VMEM((pair, ch), BF16),
                        pltpu.VMEM((n_chunks, d, ch), F32)],
        compiler_params=pltpu.CompilerParams(
            dimension_semantics=("arbitrary", "arbitrary"), vmem_limit_bytes=VMEM_LIMIT_BYTES),
        name="peer_final" if final else "peer",
    )(h2, q, x, gate, keys, u, vt, fg)


TOKEN_TILE = 512
PEER_TILE = 512
PEER_EXPERT_BLOCK = 2048
PEER_UNIT_ROWS = 4
PEER_J_TILES = 4


def kernel(x_prompt, x_sample, c_prompt, c_sample, state_conv_b, state_conv_c, norm_g, final_g, ada_w, ada_b, w_in, ln_a_g, ln_a_b, w_spatial, b_spatial, conv_b_w, conv_c_w, conv_c_b, ln_c_g, ln_c_b, w_branch, w_out, peer_w_query, peer_sub_keys, peer_u, peer_v):
    depth = w_in.shape[0]
    bp, seq, d = x_prompt.shape
    ns = x_sample.shape[0]
    dh = d // 2
    groups, chunk = w_spatial.shape[1], w_spatial.shape[2]
    gdim = dh // groups

    mod_all = _mod_call(jnp.concatenate([c_prompt, c_sample], axis=0), ada_w, ada_b)

    tril = jnp.tril(jnp.ones((chunk, chunk), dtype=bool))
    ws_all = jnp.where(tril[None, None], w_spatial, 0.0).astype(BF16)
    bs_all = jnp.repeat(jnp.swapaxes(b_spatial, 1, 2), gdim, axis=2)
    w00_all = jnp.repeat(w_spatial[:, :, 0, 0], gdim, axis=1).astype(BF16)[:, None, :]
    b0_all = jnp.repeat(b_spatial[:, :, 0], gdim, axis=1)[:, None, :]
    win_all = w_in.astype(BF16)
    wbr_all = w_branch.astype(BF16)
    wout_all = w_out.astype(BF16)
    wq_all = peer_w_query.astype(BF16)
    keys_all = peer_sub_keys.astype(BF16)
    u_all = peer_u.astype(BF16)
    n_exp = peer_u.shape[1]
    pair = 2 * PEER_UNIT_ROWS * peer_sub_keys.shape[3]
    vt_all = jnp.swapaxes(peer_v.reshape(depth, n_exp // pair, pair, d), 2, 3).astype(BF16)
    hc_hist_all = jnp.swapaxes(state_conv_c, 1, 2)
    fg = final_g[None, :]

    xp = x_prompt
    xs = x_sample.reshape(ns, d)
    v_s, hb_p, hb_s, hc_p, hc_s = [], [], [], [], []
    for l in range(depth):
        lw = dict(layer=l, ng=norm_g[l], win=_Stacked(win_all), lnag=ln_a_g[l][None], lnab=ln_a_b[l][None],
                  ws=ws_all[l], bs=bs_all[l], w00=w00_all[l], b0=b0_all[l], cbw=conv_b_w[l], ccw=conv_c_w[l],
                  ccb=conv_c_b[l][None], lncg=ln_c_g[l][None], lncb=ln_c_b[l][None], wbr=_Stacked(wbr_all),
                  wout=_Stacked(wout_all), wq=_Stacked(wq_all))
        mod_p, mod_s = mod_all[l, :bp], mod_all[l, bp:]
        final = l == depth - 1

        xmid, h2, q, hb, hc = _k1_prompt_call(xp, mod_p, lw, tt=TOKEN_TILE)
        hb_p.append(hb)
        hc_p.append(hc)
        xp = _peer_call(h2.reshape(bp * seq, d), q.reshape(bp * seq, -1), xmid.reshape(bp * seq, d),
                        mod_p[:, None, 5 * d:], keys_all, u_all, vt_all, fg,
                        layer=l, tile=PEER_TILE, eb=PEER_EXPERT_BLOCK, final=final).reshape(bp, seq, d)

        xmid, h2, q, va, ci, glu = _k1_sample_call(xs, mod_s, lw, state_conv_b[l, :, 0], state_conv_b[l, :, 1],
                                                   hc_hist_all[l])
        v_s.append(va[:, None, :])
        hb_s.append(jnp.stack([state_conv_b[l, :, 1], ci], axis=1))
        hc_s.append(jnp.concatenate([state_conv_c[l, :, 1:], glu[:, None, :]], axis=1))
        xs = _peer_call(h2[0], q[0], xmid[0], mod_s[None, :, 5 * d:], keys_all, u_all, vt_all, fg,
                        layer=l, tile=ns, eb=PEER_EXPERT_BLOCK, final=final)

    return (xp, xs.reshape(ns, 1, d), jnp.stack(v_s), jnp.stack(hb_p), jnp.stack(hb_s), jnp.stack(hc_p),
            jnp.stack(hc_s))
```

```python
import functools

import jax
import jax.numpy as jnp
from jax import lax
from jax.experimental import pallas as pl
from jax.experimental.pallas import tpu as pltpu

F32 = jnp.float32
BF16 = jnp.bfloat16
EPS = 1e-6

VMEM_LIMIT_BYTES = 56 * 1024 * 1024
LANES = 128
SUBLANES = 8
MXU_COLS = 256

PEER_TOPK = 16
CONV_ROWS = 32


def _nt_dot(a, b):
    return lax.dot_general(a, b, (((1,), (1,)), ((), ())), preferred_element_type=F32)


def _dot(a, b):
    return jnp.dot(a, b, preferred_element_type=F32)


def _rms(x, g):
    return x * lax.rsqrt(jnp.mean(x * x, axis=-1, keepdims=True) + EPS) * g


def _ln(x, g, b):
    mu = jnp.mean(x, axis=-1, keepdims=True)
    xc = x - mu
    var = jnp.mean(xc * xc, axis=-1, keepdims=True)
    return xc * lax.rsqrt(var + EPS) * g + b


def _gelu(x):
    return 0.5 * x * (1.0 + jnp.tanh(0.7978845608028654 * (x + 0.044715 * (x * x * x))))


def _sigmoid(x):
    return 1.0 / (1.0 + jnp.exp(-x))


def _const_spec(shape):
    nd = len(shape)
    return pl.BlockSpec(shape, lambda *_: (0,) * nd, pipeline_mode=pl.Buffered(1))


class _Stacked:
    def __init__(self, array):
        self.array = array
        self.shape = array.shape[1:]


def _weight_spec(w, layer):
    return _layer_spec(w.array.shape, layer) if isinstance(w, _Stacked) else _const_spec(w.shape)


def _unstack(ws):
    return [w.array if isinstance(w, _Stacked) else w for w in ws]


def _layer_spec(stacked_shape, layer):
    nd = len(stacked_shape)
    return pl.BlockSpec((None,) + tuple(stacked_shape[1:]), lambda *_: (layer,) + (0,) * (nd - 1),
                        pipeline_mode=pl.Buffered(1))


def _mod_kernel(c_ref, w_ref, b_ref, o_ref):
    c = c_ref[...]
    s = (c * _sigmoid(c)).astype(BF16)
    o_ref[0] = _dot(s, w_ref[0].astype(BF16)) + b_ref[0]


def _mod_call(c_all, ada_w, ada_b):
    depth, d, n = ada_w.shape
    m = c_all.shape[0]
    tn = 1536
    return pl.pallas_call(
        _mod_kernel,
        grid=(depth, n // tn),
        in_specs=[
            pl.BlockSpec((m, d), lambda l, j: (0, 0)),
            pl.BlockSpec((1, d, tn), lambda l, j: (l, 0, j)),
            pl.BlockSpec((1, 1, tn), lambda l, j: (l, 0, j)),
        ],
        out_specs=pl.BlockSpec((1, m, tn), lambda l, j: (l, 0, j)),
        out_shape=jax.ShapeDtypeStruct((depth, m, n), F32),
        compiler_params=pltpu.CompilerParams(
            dimension_semantics=("arbitrary", "arbitrary"), vmem_limit_bytes=VMEM_LIMIT_BYTES),
        name="adaln_mod",
    )(c_all, ada_w, ada_b.reshape(depth, 1, n))


def _z_slices(d):
    h = d // 2
    names = ["ua", "va", "hb", "bb", "cb", "ac", "gc"]
    out, lo = {}, 0
    for n in names:
        out[n] = (lo, lo + h)
        lo += h
    for k in range(3):
        out[f"g{k}"] = (lo, lo + d)
        lo += d
    return out


def _token_tail(x, merged, mod, ng_ref, wout_ref, wq_ref, xmid_ref, h2_ref, q_ref, d):
    gate1 = mod[:, 2 * d:3 * d]
    shift2, scale2 = mod[:, 3 * d:4 * d], mod[:, 4 * d:5 * d]
    xm = x + gate1 * _dot(merged.astype(BF16), wout_ref[...])
    h2 = (_rms(xm, ng_ref[1:2, :]) * (1.0 + scale2) + shift2).astype(BF16)
    xmid_ref[0] = xm
    h2_ref[0] = h2
    q_ref[0] = _dot(h2, wq_ref[...]).astype(BF16)


def _k1_prompt_kernel(x_ref, mod_ref, ng_ref, win_ref, lnag_ref, lnab_ref, ws_ref, bs_ref, cbw_ref, ccw_ref,
                      ccb_ref, lncg_ref, lncb_ref, wbr_ref, wout_ref, wq_ref,
                      xmid_ref, h2_ref, q_ref, hb_ref, hc_ref,
                      ci_buf, glu_buf, yc_buf, mix_buf, *, tt, d):
    t = pl.program_id(1)
    dh = d // 2
    zs = _z_slices(d)
    taps_c = ccw_ref.shape[0]

    @pl.when(t == 0)
    def _():
        ci_buf[0:8, :] = jnp.zeros((8, dh), F32)
        glu_buf[0:32, :] = jnp.zeros((32, dh), F32)

    x = x_ref[0]
    mod = mod_ref[0]
    shift1, scale1 = mod[:, 0:d], mod[:, d:2 * d]
    hb16 = (_rms(x, ng_ref[0:1, :]) * (1.0 + scale1) + shift1).astype(BF16)

    def z(name):
        lo, hi = zs[name]
        return _dot(hb16, win_ref[:, lo:hi])

    ci = z("cb") * z("hb")
    ci_buf[8:8 + tt, :] = ci
    conv_b = (cbw_ref[0:1, :] * ci_buf[6:6 + tt, :] + cbw_ref[1:2, :] * ci_buf[7:7 + tt, :]
              + cbw_ref[2:3, :] * ci)
    yb = z("bb") * conv_b
    merged = _sigmoid(z("g1")) * _dot(yb.astype(BF16), wbr_ref[1])

    glu_buf[32:32 + tt, :] = z("ac") * _sigmoid(z("gc"))
    for r in range(tt // CONV_ROWS):
        base = r * CONV_ROWS + 32 - (taps_c - 1)
        acc = ccb_ref[...] + ccw_ref[0:1, :] * glu_buf[base:base + CONV_ROWS, :]
        for k in range(1, taps_c):
            acc = acc + ccw_ref[k:k + 1, :] * glu_buf[base + k:base + k + CONV_ROWS, :]
        y = _ln(acc, lncg_ref[...], lncb_ref[...])
        yc_buf[r * CONV_ROWS:(r + 1) * CONV_ROWS, :] = y * _sigmoid(y)
    merged = merged + _sigmoid(z("g2")) * _dot(yc_buf[...].astype(BF16), wbr_ref[2])

    chunk = ws_ref.shape[1]
    gdim = dh // ws_ref.shape[0]
    va = _ln(_gelu(z("va")), lnag_ref[...], lnab_ref[...]).astype(BF16)
    for c in range(tt // chunk):
        for g in range(ws_ref.shape[0]):
            mix_buf[c * chunk:(c + 1) * chunk, g * gdim:(g + 1) * gdim] = (
                _dot(ws_ref[g], va[c * chunk:(c + 1) * chunk, g * gdim:(g + 1) * gdim])
                + bs_ref[:, g * gdim:(g + 1) * gdim])
    ya = _gelu(z("ua")) * mix_buf[...]
    merged = merged + _sigmoid(z("g0")) * _dot(ya.astype(BF16), wbr_ref[0])

    _token_tail(x, merged, mod, ng_ref, wout_ref, wq_ref, xmid_ref, h2_ref, q_ref, d)

    @pl.when(t == pl.num_programs(1) - 1)
    def _():
        hb_ref[0] = ci_buf[tt + 6:tt + 8, :]
        hc_ref[0] = glu_buf[tt + 2:tt + 32, :]

    ci_buf[0:8, :] = ci_buf[tt:tt + 8, :]
    glu_buf[0:32, :] = glu_buf[tt:tt + 32, :]


def _k1_sample_kernel(x_ref, mod_ref, ng_ref, win_ref, lnag_ref, lnab_ref, w00_ref, b0_ref, cbw_ref, ccw_ref,
                      ccb_ref, lncg_ref, lncb_ref, wbr_ref, wout_ref, wq_ref, hb0_ref, hb1_ref, hc_hist_ref,
                      xmid_ref, h2_ref, q_ref, va_ref, ci_ref, glu_ref, *, d):
    zs = _z_slices(d)
    taps_c = ccw_ref.shape[0]
    x = x_ref[0]
    mod = mod_ref[0]
    shift1, scale1 = mod[:, 0:d], mod[:, d:2 * d]
    hb16 = (_rms(x, ng_ref[0:1, :]) * (1.0 + scale1) + shift1).astype(BF16)

    def z(name):
        lo, hi = zs[name]
        return _dot(hb16, win_ref[:, lo:hi])

    ci = z("cb") * z("hb")
    ci_ref[...] = ci
    conv_b = cbw_ref[0:1, :] * hb0_ref[...] + cbw_ref[1:2, :] * hb1_ref[...] + cbw_ref[2:3, :] * ci
    yb = z("bb") * conv_b
    merged = _sigmoid(z("g1")) * _dot(yb.astype(BF16), wbr_ref[1])

    glu = z("ac") * _sigmoid(z("gc"))
    glu_ref[...] = glu
    acc = ccb_ref[...] + ccw_ref[taps_c - 1:taps_c, :] * glu
    for k in range(taps_c - 1):
        acc = acc + ccw_ref[k:k + 1, :] * hc_hist_ref[k]
    y = _ln(acc, lncg_ref[...], lncb_ref[...])
    yc = y * _sigmoid(y)
    merged = merged + _sigmoid(z("g2")) * _dot(yc.astype(BF16), wbr_ref[2])

    va = _ln(_gelu(z("va")), lnag_ref[...], lnab_ref[...])
    va_ref[...] = va
    mixed = w00_ref[...].astype(F32) * va.astype(BF16).astype(F32) + b0_ref[...]
    ya = _gelu(z("ua")) * mixed
    merged = merged + _sigmoid(z("g0")) * _dot(ya.astype(BF16), wbr_ref[0])

    _token_tail(x, merged, mod, ng_ref, wout_ref, wq_ref, xmid_ref, h2_ref, q_ref, d)


def _k1_prompt_call(x, mod, lw, *, tt):
    b, t, d = x.shape
    dh = d // 2
    dq = lw["wq"].shape[-1]
    kern = functools.partial(_k1_prompt_kernel, tt=tt, d=d)
    consts = [lw["ng"], lw["win"], lw["lnag"], lw["lnab"], lw["ws"], lw["bs"], lw["cbw"], lw["ccw"], lw["ccb"],
              lw["lncg"], lw["lncb"], lw["wbr"], lw["wout"], lw["wq"]]
    return pl.pallas_call(
        kern,
        grid=(b, t // tt),
        in_specs=[pl.BlockSpec((1, tt, d), lambda i, j: (i, j, 0)),
                  pl.BlockSpec((1, 1, mod.shape[-1]), lambda i, j: (i, 0, 0))]
        + [_weight_spec(c, lw["layer"]) for c in consts],
        out_specs=[pl.BlockSpec((1, tt, d), lambda i, j: (i, j, 0)),
                   pl.BlockSpec((1, tt, d), lambda i, j: (i, j, 0)),
                   pl.BlockSpec((1, tt, dq), lambda i, j: (i, j, 0)),
                   pl.BlockSpec((1, 2, dh), lambda i, j: (i, 0, 0)),
                   pl.BlockSpec((1, 30, dh), lambda i, j: (i, 0, 0))],
        out_shape=[jax.ShapeDtypeStruct((b, t, d), F32),
                   jax.ShapeDtypeStruct((b, t, d), BF16),
                   jax.ShapeDtypeStruct((b, t, dq), BF16),
                   jax.ShapeDtypeStruct((b, 2, dh), F32),
                   jax.ShapeDtypeStruct((b, 30, dh), F32)],
        scratch_shapes=[pltpu.VMEM((tt + 8, dh), F32), pltpu.VMEM((tt + 32, dh), F32),
                        pltpu.VMEM((tt, dh), F32), pltpu.VMEM((tt, dh), F32)],
        compiler_params=pltpu.CompilerParams(
            dimension_semantics=("arbitrary", "arbitrary"), vmem_limit_bytes=VMEM_LIMIT_BYTES),
        name="token_prompt",
    )(x, mod.reshape(b, 1, -1), *_unstack(consts))


def _k1_sample_call(x, mod, lw, hb0, hb1, hc_hist):
    n, d = x.shape
    dh = d // 2
    dq = lw["wq"].shape[-1]
    kern = functools.partial(_k1_sample_kernel, d=d)
    consts = [lw["ng"], lw["win"], lw["lnag"], lw["lnab"], lw["w00"], lw["b0"], lw["cbw"], lw["ccw"], lw["ccb"],
              lw["lncg"], lw["lncb"], lw["wbr"], lw["wout"], lw["wq"], hb0, hb1, hc_hist]
    row = lambda w: pl.BlockSpec((n, w), lambda i: (0, 0))
    return pl.pallas_call(
        kern,
        grid=(1,),
        in_specs=[pl.BlockSpec((1, n, d), lambda i: (0, 0, 0)),
                  pl.BlockSpec((1, n, mod.shape[-1]), lambda i: (0, 0, 0))]
        + [_weight_spec(c, lw["layer"]) for c in consts],
        out_specs=[pl.BlockSpec((1, n, d), lambda i: (0, 0, 0)),
                   pl.BlockSpec((1, n, d), lambda i: (0, 0, 0)),
                   pl.BlockSpec((1, n, dq), lambda i: (0, 0, 0)),
                   row(dh), row(dh), row(dh)],
        out_shape=[jax.ShapeDtypeStruct((1, n, d), F32),
                   jax.ShapeDtypeStruct((1, n, d), BF16),
                   jax.ShapeDtypeStruct((1, n, dq), BF16),
                   jax.ShapeDtypeStruct((n, dh), F32),
                   jax.ShapeDtypeStruct((n, dh), F32),
                   jax.ShapeDtypeStruct((n, dh), F32)],
        compiler_params=pltpu.CompilerParams(
            dimension_semantics=("arbitrary",), vmem_limit_bytes=VMEM_LIMIT_BYTES),
        name="token_sample",
    )(x[None], mod[None], *_unstack(consts))


GELU_C0 = 0.7978845608028654
GELU_C1 = GELU_C0 * 0.044715
TAU_MARGIN = 2.0 ** -21


def _aligned(x, m):
    return x if isinstance(x, int) else pl.multiple_of(x, m)


def _oddeven_mergesort_pairs(n):
    pairs, p = [], 1
    while p < n:
        k = p
        while k >= 1:
            for j in range(k % p, n - k, 2 * k):
                for i in range(min(k, n - j - k)):
                    if (i + j) // (2 * p) == (i + j + k) // (2 * p):
                        pairs.append((i + j, i + j + k))
            k //= 2
        p *= 2
    return pairs


def _sort_desc(xs):
    xs = list(xs)
    for i, j in _oddeven_mergesort_pairs(len(xs)):
        xs[i], xs[j] = jnp.maximum(xs[i], xs[j]), jnp.minimum(xs[i], xs[j])
    return xs


def _merge_top(xs, ys):
    n = len(xs)
    zs = [jnp.maximum(xs[k], ys[n - 1 - k]) for k in range(n)]
    d = n // 2
    while d >= 1:
        for k in range(n):
            if k & d == 0:
                zs[k], zs[k + d] = jnp.maximum(zs[k], zs[k + d]), jnp.minimum(zs[k], zs[k + d])
        d //= 2
    return zs


def _peer_kernel(h2_ref, q_ref, x_ref, gate_ref, keys_ref, u_ref, vt_ref, fg_ref, out_ref,
                 th_s, s2_s, e1_s, e2_s, top_s, rowb_s, a0_s, a1_s, wa0_s, wa1_s, wa2_s, wa3_s, acc_s,
                 *, ch, final, unit):
    e = pl.program_id(1)
    n_heads, n_chunks, n_keys, _ = th_s.shape
    rows = unit // n_keys
    rows_per_block = u_ref.shape[0] // n_keys
    upc = u_ref.shape[0] // unit
    n_units = n_chunks * upc
    n_pairs = n_units // 2
    assert upc % 2 == 0 and SUBLANES % rows == 0
    tiles_per_row = n_keys // SUBLANES
    assert tiles_per_row == PEER_TOPK
    neg_inf = jnp.float32(-jnp.inf)

    @pl.when(e == 0)
    def _scores_and_thresholds():
        acc_s[...] = jnp.zeros(acc_s.shape, F32)
        for c in range(n_chunks):
            qc = q_ref[c * ch:(c + 1) * ch, :]
            for h in range(n_heads):
                for p in range(2):
                    col = (2 * h + p) * n_keys
                    s = _nt_dot(keys_ref[h, p], qc[:, col:col + n_keys])
                    if p == 0:
                        th_s[h, c] = s
                    else:
                        s2_s[h, c] = s.reshape(tiles_per_row, SUBLANES, ch)
                    xs = _sort_desc([s[k * SUBLANES:(k + 1) * SUBLANES, :] for k in range(tiles_per_row)])
                    for shift in (4, 2, 1):
                        xs = _merge_top(xs, [pltpu.roll(v, shift, 0) for v in xs])
                    for r in range(PEER_TOPK):
                        top_s[p, r, h:h + 1, :] = xs[r][0:1, :]
            a = [top_s[0, r] for r in range(PEER_TOPK)]
            b = [top_s[1, r] for r in range(PEER_TOPK)]
            pad = jnp.full(a[0].shape, neg_inf, F32)
            lists = [[a[r] + b[s] for r in range(PEER_TOPK // (s + 1))] for s in range(PEER_TOPK // 2)]
            lists.append([a[0] + b[s] for s in range(PEER_TOPK // 2, PEER_TOPK)])
            lists = [l + [pad] * (PEER_TOPK - len(l)) for l in lists]
            best = functools.reduce(_merge_top, lists)
            top = best[0]
            tau = best[PEER_TOPK - 1] - jnp.abs(best[PEER_TOPK - 1]) * TAU_MARGIN
            zsum = functools.reduce(lambda acc, v: acc + jnp.exp(v - top), best[1:], jnp.ones_like(top))
            half_zinv = 0.5 / zsum
            for h in range(n_heads):
                hs = slice(h, h + 1)
                s1 = th_s[h, c]
                e1_s[h, c] = jnp.exp(s1 - a[0][hs, :])
                th_s[h, c] = tau[hs, :] - s1
                e2_s[h, c] = jnp.exp(s2_s[h, c] - b[0][hs, :]) * half_zinv[hs, :]

    pairs_per_chunk = upc // 2
    a_bufs = (a0_s, a1_s)
    wa_bufs = (wa0_s, wa1_s, wa2_s, wa3_s)

    def stage_a(c, lo, hi):
        prod = _nt_dot(u_ref[lo:hi, :], h2_ref[c * ch:(c + 1) * ch, :])
        a_bufs[c % 2][lo // SUBLANES:hi // SUBLANES] = prod.reshape((hi - lo) // SUBLANES, SUBLANES, ch)

    def stage_b(c, k, wa_s, half):
        a_s = a_bufs[c % 2]
        i8 = pl.multiple_of(e * rows_per_block + (k * rows // SUBLANES) * SUBLANES, SUBLANES)
        to_top = (SUBLANES - (k * rows) % SUBLANES) % SUBLANES
        ub = k % rowb_s.shape[0]
        for h in range(n_heads):
            for which, ref in enumerate((th_s, e1_s)):
                tile = ref[h, c, pl.ds(i8, SUBLANES), :]
                if to_top:
                    tile = pltpu.roll(tile, to_top, 0)
                for r in range(rows):
                    rowb_s[ub, which, r, h] = jnp.broadcast_to(tile[r:r + 1, :], (SUBLANES, ch))
        for lt in range(ch // LANES):
            ls = slice(lt * LANES, (lt + 1) * LANES)
            for jq in range(tiles_per_row // PEER_J_TILES):
                js = slice(jq * PEER_J_TILES, (jq + 1) * PEER_J_TILES)
                wsum = [None] * rows
                for h in range(n_heads):
                    s2t = s2_s[h, c, js, :, ls]
                    e2t = e2_s[h, c, js, :, ls]
                    for r in range(rows):
                        keep = s2t >= rowb_s[ub, 0, r, h, :, ls]
                        term = rowb_s[ub, 1, r, h, :, ls] * jnp.where(keep, e2t, 0.0)
                        wsum[r] = term if wsum[r] is None else wsum[r] + term
                for r in range(rows):
                    t0 = (k * rows + r) * tiles_per_row + jq * PEER_J_TILES
                    av = a_s[t0:t0 + PEER_J_TILES, :, ls]
                    y = av * (1.0 + jnp.tanh(av * (GELU_C0 + GELU_C1 * (av * av))))
                    r0 = half * unit + (r * tiles_per_row + jq * PEER_J_TILES) * SUBLANES
                    wa_s[r0:r0 + PEER_J_TILES * SUBLANES, ls] = (
                        (wsum[r] * y).reshape(PEER_J_TILES * SUBLANES, LANES).astype(BF16))

    def stage_c(c, kp, wa_s):
        acc_s[c] = acc_s[c] + _dot(vt_ref[kp], wa_s[...])

    n_block = u_ref.shape[0]
    stage_a(0, 0, PEER_HEAD_UNITS * unit)
    prev = None
    for c in range(n_chunks):
        def chunk_block(c=c, prev=prev):
            for kp in range(pairs_per_chunk):
                wa_s = wa_bufs[(c * pairs_per_chunk + kp) % len(wa_bufs)]
                if kp == 0 and c == 0 and PEER_HEAD_UNITS * unit < n_block:
                    stage_a(0, PEER_HEAD_UNITS * unit, n_block)
                if kp == 0 and c + 1 < n_chunks:
                    stage_a(c + 1, 0, n_block)
                if prev is not None:
                    stage_c(*prev)
                stage_b(c, 2 * kp, wa_s, 0)
                stage_b(c, 2 * kp + 1, wa_s, 1)
                prev = (c, kp, wa_s)

        pl.when(e >= 0)(chunk_block)
        last = (c + 1) * pairs_per_chunk - 1
        prev = (c, pairs_per_chunk - 1, wa_bufs[last % len(wa_bufs)])
    stage_c(*prev)

    @pl.when(e == pl.num_programs(1) - 1)
    def _epilogue():
        for c in range(n_chunks):
            tok = slice(c * ch, (c + 1) * ch)
            y = x_ref[tok, :] + gate_ref[0] * acc_s[c].T
            if final:
                y = _rms(y, fg_ref[...])
            out_ref[tok, :] = y


def _peer_call(h2, q, x, gate, keys, u, vt, fg, *, layer, tile, eb, final):
    n, d = x.shape
    _, n_heads, _, n_keys, _ = keys.shape
    n_exp = u.shape[1]
    pair = vt.shape[3]
    unit = pair // 2
    ch = min(MXU_COLS, tile)
    n_chunks = tile // ch
    groups, grows, _ = gate.shape
    tiles_per_group = (n // tile) // groups
    kern = functools.partial(_peer_kernel, ch=ch, final=final, unit=unit)
    row_scratch = pltpu.VMEM((n_heads, n_chunks, n_keys, ch), F32)
    tile_scratch = pltpu.VMEM((n_heads, n_chunks, n_keys // SUBLANES, SUBLANES, ch), F32)
    prod_scratch = pltpu.VMEM((eb // SUBLANES, SUBLANES, ch), F32)
    return pl.pallas_call(
        kern,
        grid=(n // tile, n_exp // eb),
        in_specs=[pl.BlockSpec((tile, d), lambda t, e: (t, 0)),
                  pl.BlockSpec((tile, q.shape[1]), lambda t, e: (t, 0)),
                  pl.BlockSpec((tile, d), lambda t, e: (t, 0)),
                  pl.BlockSpec((1, grows, d), lambda t, e: (t // tiles_per_group, 0, 0)),
                  _layer_spec(keys.shape, layer),
                  pl.BlockSpec((None, eb, d), lambda t, e: (layer, e, 0)),
                  pl.BlockSpec((None, eb // pair, d, pair), lambda t, e: (layer, e, 0, 0)),
                  _const_spec(fg.shape)],
        out_specs=pl.BlockSpec((tile, d), lambda t, e: (t, 0)),
        out_shape=jax.ShapeDtypeStruct((n, d), F32),
        scratch_shapes=[row_scratch, tile_scratch, row_scratch, tile_scratch,
                        pltpu.VMEM((2, PEER_TOPK, n_heads, ch), F32),
                        pltpu.VMEM((eb // unit, 2, unit // n_keys, n_heads, SUBLANES, ch), F32),
                        prod_scratch, prod_scratch,
                        pltpu.VMEM((pair, ch), BF16), pltpu.VMEM((pair, ch), BF16),
                        pltpu.VMEM((pair, ch), BF16), pltpu.VMEM((pair, ch), BF16),
                        pltpu.VMEM((n_chunks, d, ch), F32)],
        compiler_params=pltpu.CompilerParams(
            dimension_semantics=("arbitrary", "arbitrary"), vmem_limit_bytes=VMEM_LIMIT_BYTES),
        name="peer_final" if final else "peer",
    )(h2, q, x, gate, keys, u, vt, fg)


TOKEN_TILE = 512
PEER_TILE = 512
PEER_EXPERT_BLOCK = 2048
PEER_UNIT_ROWS = 4
PEER_J_TILES = 4
PEER_HEAD_UNITS = 2


def kernel(x_prompt, x_sample, c_prompt, c_sample, state_conv_b, state_conv_c, norm_g, final_g, ada_w, ada_b, w_in, ln_a_g, ln_a_b, w_spatial, b_spatial, conv_b_w, conv_c_w, conv_c_b, ln_c_g, ln_c_b, w_branch, w_out, peer_w_query, peer_sub_keys, peer_u, peer_v):
    depth = w_in.shape[0]
    bp, seq, d = x_prompt.shape
    ns = x_sample.shape[0]
    dh = d // 2
    groups, chunk = w_spatial.shape[1], w_spatial.shape[2]
    gdim = dh // groups

    mod_all = _mod_call(jnp.concatenate([c_prompt, c_sample], axis=0), ada_w, ada_b)

    tril = jnp.tril(jnp.ones((chunk, chunk), dtype=bool))
    ws_all = jnp.where(tril[None, None], w_spatial, 0.0).astype(BF16)
    bs_all = jnp.repeat(jnp.swapaxes(b_spatial, 1, 2), gdim, axis=2)
    w00_all = jnp.repeat(w_spatial[:, :, 0, 0], gdim, axis=1).astype(BF16)[:, None, :]
    b0_all = jnp.repeat(b_spatial[:, :, 0], gdim, axis=1)[:, None, :]
    win_all = w_in.astype(BF16)
    wbr_all = w_branch.astype(BF16)
    wout_all = w_out.astype(BF16)
    wq_all = peer_w_query.astype(BF16)
    keys_all = peer_sub_keys.astype(BF16)
    u_all = peer_u.astype(BF16)
    n_exp = peer_u.shape[1]
    pair = 2 * PEER_UNIT_ROWS * peer_sub_keys.shape[3]
    vt_all = jnp.swapaxes(peer_v.reshape(depth, n_exp // pair, pair, d), 2, 3).astype(BF16)
    hc_hist_all = jnp.swapaxes(state_conv_c, 1, 2)
    fg = final_g[None, :]

    xp = x_prompt
    xs = x_sample.reshape(ns, d)
    v_s, hb_p, hb_s, hc_p, hc_s = [], [], [], [], []
    for l in range(depth):
        lw = dict(layer=l, ng=norm_g[l], win=_Stacked(win_all), lnag=ln_a_g[l][None], lnab=ln_a_b[l][None],
                  ws=ws_all[l], bs=bs_all[l], w00=w00_all[l], b0=b0_all[l], cbw=conv_b_w[l], ccw=conv_c_w[l],
                  ccb=conv_c_b[l][None], lncg=ln_c_g[l][None], lncb=ln_c_b[l][None], wbr=_Stacked(wbr_all),
                  wout=_Stacked(wout_all), wq=_Stacked(wq_all))
        mod_p, mod_s = mod_all[l, :bp], mod_all[l, bp:]
        final = l == depth - 1

        xmid, h2, q, hb, hc = _k1_prompt_call(xp, mod_p, lw, tt=TOKEN_TILE)
        hb_p.append(hb)
        hc_p.append(hc)
        xp = _peer_call(h2.reshape(bp * seq, d), q.reshape(bp * seq, -1), xmid.reshape(bp * seq, d),
                        mod_p[:, None, 5 * d:], keys_all, u_all, vt_all, fg,
                        layer=l, tile=PEER_TILE, eb=PEER_EXPERT_BLOCK, final=final).reshape(bp, seq, d)

        xmid, h2, q, va, ci, glu = _k1_sample_call(xs, mod_s, lw, state_conv_b[l, :, 0], state_conv_b[l, :, 1],
                                                   hc_hist_all[l])
        v_s.append(va[:, None, :])
        hb_s.append(jnp.stack([state_conv_b[l, :, 1], ci], axis=1))
        hc_s.append(jnp.concatenate([state_conv_c[l, :, 1:], glu[:, None, :]], axis=1))
        xs = _peer_call(h2[0], q[0], xmid[0], mod_s[None, :, 5 * d:], keys_all, u_all, vt_all, fg,
                        layer=l, tile=ns, eb=PEER_EXPERT_BLOCK, final=final)

    return (xp, xs.reshape(ns, 1, d), jnp.stack(v_s), jnp.stack(hb_p), jnp.stack(hb_s), jnp.stack(hc_p),
            jnp.stack(hc_s))
```
